```python
import math
import jax, jax.numpy as jnp
from jax import lax
import numpy as np

D_MODEL = 2048
BATCH = 4
SEQ = 4096
DEPTH = 2

HEAD_DIM = 128
MOBA_HEADS = 8
MOBA_BLOCK = 256
MOBA_TOPK = 3
MOBA_Q_CHUNK = 32
SB_HEADS = 8
SB_Q_BLOCK = 128
POOL_WINDOWS = (2, 4, 8, 16)
POOL_GROUPS = len(POOL_WINDOWS)
POOL_GROUP = 256
POOL_WIDTH = POOL_GROUP * POOL_GROUPS
MOBA_WIDTH = MOBA_HEADS * HEAD_DIM
SB_WIDTH = SB_HEADS * HEAD_DIM
N_BRANCH = 3
IN_WIDTH = 3 * MOBA_WIDTH + POOL_WIDTH + 3 * SB_WIDTH + N_BRANCH * D_MODEL
D_FF = -(-8 * D_MODEL // (3 * 256)) * 256
REL_BUCKETS = 32
REL_MAX_EXACT = 16
REL_MAX_DIST = 2048
EPS = 1e-6
NEG = -1e30

kernel_name = "hybrid_moba_pool_stickbreak_block"


def rms_norm(x, g):
    xf = x.astype(jnp.float32)
    y = xf * lax.rsqrt(jnp.mean(xf * xf, axis=-1, keepdims=True) + EPS)
    return (y * g.astype(jnp.float32)).astype(x.dtype)


def rel_bucket(dist):
    n = jnp.maximum(dist, 0)
    nf = jnp.maximum(n, 1).astype(jnp.float32)
    large = REL_MAX_EXACT + (jnp.log(nf / REL_MAX_EXACT) / math.log(REL_MAX_DIST / REL_MAX_EXACT)
                             * (REL_BUCKETS - REL_MAX_EXACT)).astype(jnp.int32)
    large = jnp.minimum(large, REL_BUCKETS - 1)
    return jnp.where(n < REL_MAX_EXACT, n, large)


def moba_attention(q, k, v, rel_table):
    B, H, S, Dh = q.shape
    nb = -(-S // MOBA_BLOCK)
    pad = nb * MOBA_BLOCK - S
    kb = jnp.pad(k, ((0, 0), (0, 0), (0, pad), (0, 0))).reshape(B, H, nb, MOBA_BLOCK, Dh)
    vb = jnp.pad(v, ((0, 0), (0, 0), (0, pad), (0, 0))).reshape(B, H, nb, MOBA_BLOCK, Dh)
    kbar = jnp.mean(kb.astype(jnp.float32), axis=3)
    topk = min(MOBA_TOPK, nb)
    scale = Dh ** -0.5
    bi = jnp.arange(B)[:, None, None, None]
    hi = jnp.arange(H)[None, :, None, None]
    hi5 = jnp.arange(H)[None, :, None, None, None]
    blk_off = jnp.arange(MOBA_BLOCK)
    Q = MOBA_Q_CHUNK

    def chunk(c):
        t0 = c * Q
        qc = lax.dynamic_slice_in_dim(q, t0, Q, axis=2).astype(jnp.float32)
        pos = t0 + jnp.arange(Q)
        own = t0 // MOBA_BLOCK
        gate = jnp.einsum('bhqd,bhnd->bhqn', qc, kbar)
        gate = jnp.where(jnp.arange(nb) < own, gate, -jnp.inf)
        _, idx = lax.top_k(gate, topk)
        valid = jnp.arange(topk) < own
        ksel = kb[bi, hi, idx].astype(jnp.float32)
        vsel = vb[bi, hi, idx].astype(jnp.float32)
        s_sel = jnp.einsum('bhqd,bhqnkd->bhqnk', qc, ksel) * scale
        key_pos = idx[..., None] * MOBA_BLOCK + blk_off
        bias_sel = rel_table[hi5, rel_bucket(pos[:, None, None] - key_pos)].astype(jnp.float32)
        s_sel = jnp.where(valid[:, None], s_sel + bias_sel, NEG)
        k_own = lax.dynamic_index_in_dim(kb, own, axis=2, keepdims=False).astype(jnp.float32)
        v_own = lax.dynamic_index_in_dim(vb, own, axis=2, keepdims=False).astype(jnp.float32)
        d_own = pos[:, None] - (own * MOBA_BLOCK + blk_off)[None, :]
        s_own = jnp.einsum('bhqd,bhkd->bhqk', qc, k_own) * scale
        s_own = jnp.where(d_own >= 0, s_own + rel_table[:, rel_bucket(d_own)].astype(jnp.float32)[None], NEG)
        logits = jnp.concatenate([s_sel.reshape(B, H, Q, topk * MOBA_BLOCK), s_own], axis=-1)
        p = jax.nn.softmax(logits, axis=-1)
        p_sel = p[..., :topk * MOBA_BLOCK].reshape(B, H, Q, topk, MOBA_BLOCK)
        p_own = p[..., topk * MOBA_BLOCK:]
        return (jnp.einsum('bhqnk,bhqnkd->bhqd', p_sel, vsel)
                + jnp.einsum('bhqk,bhkd->bhqd', p_own, v_own))

    out = lax.map(chunk, jnp.arange(S // Q))
    return jnp.moveaxis(out, 0, 2).reshape(B, H, S, Dh).astype(q.dtype)


def stick_breaking_attention(q, k, v):
    B, H, S, Dh = q.shape
    scale = Dh ** -0.5
    outs = []
    for t0 in range(0, S, SB_Q_BLOCK):
        t1 = t0 + SB_Q_BLOCK
        qc = q[:, :, t0:t1].astype(jnp.float32)
        kc = k[:, :, :t1].astype(jnp.float32)
        vc = v[:, :, :t1].astype(jnp.float32)
        z = jnp.einsum('bhqd,bhkd->bhqk', qc, kc) * scale
        causal = jnp.arange(t1)[None, :] < jnp.arange(t0, t1)[:, None]
        log_beta = jax.nn.log_sigmoid(z)
        log_1m = jnp.where(causal, jax.nn.log_sigmoid(-z), 0.0)
        after = lax.cumsum(log_1m, axis=3, reverse=True) - log_1m
        a = jnp.where(causal, jnp.exp(log_beta + after), 0.0)
        outs.append(jnp.einsum('bhqk,bhkd->bhqd', a, vc))
    return jnp.concatenate(outs, axis=2).astype(q.dtype)


def multiscale_pool(u, w_grp, scale):
    B, S, _ = u.shape
    ug = u.astype(jnp.float32).reshape(B, S, POOL_GROUPS, POOL_GROUP)
    cs = jnp.concatenate([jnp.zeros((B, 1, POOL_GROUPS, POOL_GROUP), jnp.float32),
                          jnp.cumsum(ug, axis=1)], axis=1)
    t = jnp.arange(S)
    pooled = []
    for gi, w in enumerate(POOL_WINDOWS):
        lo = jnp.maximum(t + 1 - w, 0)
        cnt = (t + 1 - lo).astype(jnp.float32)
        mean = (cs[:, 1:, gi] - cs[:, lo, gi]) / cnt[:, None]
        pooled.append(mean - ug[:, :, gi])
    p = jnp.stack(pooled, axis=2)
    y = jnp.einsum('bsgc,gcd->bsgd', p, w_grp.astype(jnp.float32)).reshape(B, S, POOL_WIDTH)
    return (y * scale.astype(jnp.float32)).astype(u.dtype)


def hybrid_mixer(h, w_in, w_pool, pool_scale, w_br_a, w_br_b, w_br_c, w_out, rel_table):
    B, S, _ = h.shape
    proj = h @ w_in
    offs = np.cumsum([MOBA_WIDTH, MOBA_WIDTH, MOBA_WIDTH, POOL_WIDTH, SB_WIDTH, SB_WIDTH, SB_WIDTH]).tolist()
    qa, ka, va, u, qs, ks, vs, gates = jnp.split(proj, offs, axis=-1)
    to_heads = lambda t, nh: t.reshape(B, S, nh, HEAD_DIM).transpose(0, 2, 1, 3)
    from_heads = lambda t: t.transpose(0, 2, 1, 3).reshape(B, S, -1)
    ya = from_heads(moba_attention(to_heads(qa, MOBA_HEADS), to_heads(ka, MOBA_HEADS),
                                   to_heads(va, MOBA_HEADS), rel_table))
    yb = multiscale_pool(u, w_pool, pool_scale)
    yc = from_heads(stick_breaking_attention(to_heads(qs, SB_HEADS), to_heads(ks, SB_HEADS),
                                             to_heads(vs, SB_HEADS)))
    g = jax.nn.sigmoid(gates.astype(jnp.float32)).reshape(B, S, N_BRANCH, D_MODEL)
    m = (g[:, :, 0] * (ya @ w_br_a).astype(jnp.float32)
         + g[:, :, 1] * (yb @ w_br_b).astype(jnp.float32)
         + g[:, :, 2] * (yc @ w_br_c).astype(jnp.float32))
    return m.astype(h.dtype) @ w_out


def swiglu(h, w_gate, w_up, w_down):
    return (jax.nn.silu(h @ w_gate) * (h @ w_up)) @ w_down


def setup_inputs(seed: int = 0) -> dict:
    key = jax.random.key(seed)
    ks = jax.random.split(key, 16)
    nrm = lambda k, shape, fan: jax.random.normal(k, shape, jnp.float32) * (fan ** -0.5)
    return {
        "x": jax.random.normal(ks[0], (BATCH, SEQ, D_MODEL), jnp.float32),
        "norm_mix": 1.0 + 0.02 * jax.random.normal(ks[1], (DEPTH, D_MODEL), jnp.float32),
        "norm_ffn": 1.0 + 0.02 * jax.random.normal(ks[2], (DEPTH, D_MODEL), jnp.float32),
        "w_in": nrm(ks[3], (DEPTH, D_MODEL, IN_WIDTH), D_MODEL),
        "w_pool": nrm(ks[4], (DEPTH, POOL_GROUPS, POOL_GROUP, POOL_GROUP), POOL_GROUP),
        "pool_scale": 1.0 + 0.02 * jax.random.normal(ks[5], (DEPTH, POOL_WIDTH), jnp.float32),
        "w_br_a": nrm(ks[6], (DEPTH, MOBA_WIDTH, D_MODEL), MOBA_WIDTH),
        "w_br_b": nrm(ks[7], (DEPTH, POOL_WIDTH, D_MODEL), POOL_WIDTH),
        "w_br_c": nrm(ks[8], (DEPTH, SB_WIDTH, D_MODEL), SB_WIDTH),
        "w_out": nrm(ks[9], (DEPTH, D_MODEL, D_MODEL), D_MODEL),
        "w_gate": nrm(ks[10], (DEPTH, D_MODEL, D_FF), D_MODEL),
        "w_up": nrm(ks[11], (DEPTH, D_MODEL, D_FF), D_MODEL),
        "w_down": nrm(ks[12], (DEPTH, D_FF, D_MODEL), D_FF),
        "rel_bias": 0.5 * jax.random.normal(ks[13], (MOBA_HEADS, REL_BUCKETS), jnp.float32),
        "norm_final": 1.0 + 0.02 * jax.random.normal(ks[14], (D_MODEL,), jnp.float32),
    }


def reference(x, norm_mix, norm_ffn, w_in, w_pool, pool_scale, w_br_a, w_br_b, w_br_c,
              w_out, w_gate, w_up, w_down, rel_bias, norm_final):
    h = x
    for l in range(DEPTH):
        h = h + hybrid_mixer(rms_norm(h, norm_mix[l]), w_in[l], w_pool[l], pool_scale[l],
                             w_br_a[l], w_br_b[l], w_br_c[l], w_out[l], rel_bias)
        h = h + swiglu(rms_norm(h, norm_ffn[l]), w_gate[l], w_up[l], w_down[l])
    return rms_norm(h, norm_final)
```

```python
import functools
import math

import jax
import jax.numpy as jnp
from jax import lax
from jax.experimental import pallas as pl
from jax.experimental.pallas import tpu as pltpu

F32 = jnp.float32
BF16 = jnp.bfloat16

HEAD_DIM = 128
MOBA_HEADS = 8
MOBA_BLOCK = 256
MOBA_TOPK = 3
SB_HEADS = 8
POOL_WINDOWS = (2, 4, 8, 16)
POOL_GROUP = 256
POOL_HALO = 16
REL_BUCKETS = 32
REL_MAX_EXACT = 16
REL_MAX_DIST = 2048
EPS = 1e-6
NEG = -1e30
EXP_ZERO_BELOW = -104.0

V7X_LANES = 128
V7X_VMEM_LIMIT_BYTES = 56 * 1024 * 1024

_NT = (((1,), (1,)), ((), ()))


def _params(*semantics):
    return pltpu.CompilerParams(dimension_semantics=semantics,
                                vmem_limit_bytes=V7X_VMEM_LIMIT_BYTES)


def _pick_tile(n, want, align):
    t = min(want, n)
    while t > align and (n % t or t % align):
        t -= align
    assert n % t == 0 and t % align == 0, (n, want, align)
    return t


def _rms_normalize(x, g):
    ms = jnp.mean(x * x, axis=-1, keepdims=True)
    return x * lax.rsqrt(ms + EPS) * g


def _norm_matmul_kernel(x_ref, g_ref, w_ref, o_ref, xn_ref):
    @pl.when(pl.program_id(1) == 0)
    def _():
        xn_ref[...] = _rms_normalize(x_ref[...], g_ref[...]).astype(BF16)

    o_ref[...] = jnp.dot(xn_ref[...], w_ref[...], preferred_element_type=F32).astype(o_ref.dtype)


def norm_matmul(x, g, w, *, out_dtype=BF16, tm=1024, tn=1024):
    T, K = x.shape
    N = w.shape[1]
    tm = _pick_tile(T, tm, 16)
    tn = _pick_tile(N, tn, V7X_LANES)
    return pl.pallas_call(
        _norm_matmul_kernel,
        out_shape=jax.ShapeDtypeStruct((T, N), out_dtype),
        grid=(T // tm, N // tn),
        in_specs=[pl.BlockSpec((tm, K), lambda i, j: (i, 0)),
                  pl.BlockSpec((1, K), lambda i, j: (0, 0)),
                  pl.BlockSpec((K, tn), lambda i, j: (0, j))],
        out_specs=pl.BlockSpec((tm, tn), lambda i, j: (i, j)),
        scratch_shapes=[pltpu.VMEM((tm, K), BF16)],
        compiler_params=_params("parallel", "arbitrary"),
        name="norm_matmul",
    )(x, g.reshape(1, K), w)


def _ffn_up_kernel(x_ref, g_ref, wg_ref, wu_ref, o_ref, xn_ref):
    @pl.when(pl.program_id(1) == 0)
    def _():
        xn_ref[...] = _rms_normalize(x_ref[...], g_ref[...]).astype(BF16)

    xn = xn_ref[...]
    a = jnp.dot(xn, wg_ref[...], preferred_element_type=F32)
    b = jnp.dot(xn, wu_ref[...], preferred_element_type=F32)
    o_ref[...] = (a * jax.nn.sigmoid(a) * b).astype(o_ref.dtype)


def ffn_up(x, g, wg, wu, *, tm=1024, tn=512):
    T, K = x.shape
    N = wg.shape[1]
    tm = _pick_tile(T, tm, 16)
    tn = _pick_tile(N, tn, V7X_LANES)
    return pl.pallas_call(
        _ffn_up_kernel,
        out_shape=jax.ShapeDtypeStruct((T, N), BF16),
        grid=(T // tm, N // tn),
        in_specs=[pl.BlockSpec((tm, K), lambda i, j: (i, 0)),
                  pl.BlockSpec((1, K), lambda i, j: (0, 0)),
                  pl.BlockSpec((K, tn), lambda i, j: (0, j)),
                  pl.BlockSpec((K, tn), lambda i, j: (0, j))],
        out_specs=pl.BlockSpec((tm, tn), lambda i, j: (i, j)),
        scratch_shapes=[pltpu.VMEM((tm, K), BF16)],
        compiler_params=_params("parallel", "arbitrary"),
        name="ffn_up",
    )(x, g.reshape(1, K), wg, wu)


def _matmul_res_kernel(x_ref, w_ref, r_ref, o_ref):
    o_ref[...] = r_ref[...] + jnp.dot(x_ref[...], w_ref[...], preferred_element_type=F32)


def matmul_res(x, w, res, *, tm=1024, tn=512):
    T, K = x.shape
    N = w.shape[1]
    tm = _pick_tile(T, tm, 16)
    tn = _pick_tile(N, tn, V7X_LANES)
    return pl.pallas_call(
        _matmul_res_kernel,
        out_shape=jax.ShapeDtypeStruct((T, N), F32),
        grid=(T // tm, N // tn),
        in_specs=[pl.BlockSpec((tm, K), lambda i, j: (i, 0)),
                  pl.BlockSpec((K, tn), lambda i, j: (0, j)),
                  pl.BlockSpec((tm, tn), lambda i, j: (i, j))],
        out_specs=pl.BlockSpec((tm, tn), lambda i, j: (i, j)),
        compiler_params=_params("parallel", "parallel"),
        name="matmul_res",
    )(x, w, res)


def _merge_kernel(ya_ref, yb_ref, yc_ref, ga_ref, gb_ref, gc_ref, wa_ref, wb_ref, wc_ref, o_ref):
    def branch(y_ref, g_ref, w_ref):
        gate = jax.nn.sigmoid(g_ref[...].astype(F32))
        return gate * jnp.dot(y_ref[...], w_ref[...], preferred_element_type=F32)

    m = branch(ya_ref, ga_ref, wa_ref) + branch(yb_ref, gb_ref, wb_ref) + branch(yc_ref, gc_ref, wc_ref)
    o_ref[...] = m.astype(o_ref.dtype)


def merge_branches(ya, yb, yc, proj, gate_off, wa, wb, wc, *, tm=1024, tn=512):
    T, K = ya.shape
    N = wa.shape[1]
    tm = _pick_tile(T, tm, 16)
    tn = _pick_tile(N, tn, V7X_LANES)
    assert gate_off % tn == 0
    y_spec = pl.BlockSpec((tm, K), lambda i, j: (i, 0))
    w_spec = pl.BlockSpec((K, tn), lambda i, j: (0, j))

    def gate_spec(branch):
        base = (gate_off + branch * N) // tn
        return pl.BlockSpec((tm, tn), lambda i, j: (i, base + j))

    return pl.pallas_call(
        _merge_kernel,
        out_shape=jax.ShapeDtypeStruct((T, N), BF16),
        grid=(T // tm, N // tn),
        in_specs=[y_spec, y_spec, y_spec, gate_spec(0), gate_spec(1), gate_spec(2),
                  w_spec, w_spec, w_spec],
        out_specs=pl.BlockSpec((tm, tn), lambda i, j: (i, j)),
        compiler_params=_params("parallel", "parallel"),
        name="merge_branches",
    )(ya, yb, yc, proj, proj, proj, wa, wb, wc)


def _final_norm_kernel(x_ref, g_ref, o_ref):
    o_ref[...] = _rms_normalize(x_ref[...], g_ref[...])


def final_norm(x, g, *, tm=512):
    T, K = x.shape
    tm = _pick_tile(T, tm, 8)
    return pl.pallas_call(
        _final_norm_kernel,
        out_shape=jax.ShapeDtypeStruct((T, K), F32),
        grid=(T // tm,),
        in_specs=[pl.BlockSpec((tm, K), lambda i: (i, 0)),
                  pl.BlockSpec((1, K), lambda i: (0, 0))],
        out_specs=pl.BlockSpec((tm, K), lambda i: (i, 0)),
        compiler_params=_params("parallel"),
        name="final_norm",
    )(x, g.reshape(1, K))


def _pool_kernel(u_ref, halo_ref, w_ref, sc_ref, o_ref, ext_ref, *, tm, tiles_per_seq):
    t_in_seq = pl.program_id(0) % tiles_per_seq
    halo = halo_ref[...].astype(F32)
    ext_ref[0:POOL_HALO, :] = jnp.where(t_in_seq == 0, 0.0, halo)
    ext_ref[POOL_HALO:, :] = u_ref[...].astype(F32)
    pos = t_in_seq * tm + lax.broadcasted_iota(jnp.int32, (tm, 1), 0)
    for gi, win in enumerate(POOL_WINDOWS):
        cols = slice(gi * POOL_GROUP, (gi + 1) * POOL_GROUP)
        x = ext_ref[POOL_HALO:, cols]
        total = x
        for back in range(1, win):
            total = total + ext_ref[pl.ds(POOL_HALO - back, tm), cols]
        cnt = jnp.minimum(pos + 1, win).astype(F32)
        pooled = total / cnt - x
        y = jnp.dot(pooled.astype(BF16), w_ref[gi], preferred_element_type=F32)
        o_ref[:, cols] = (y * sc_ref[:, cols]).astype(o_ref.dtype)


def pool_mixer(proj, u_off, w_grp, scale, seq_len, *, tm=512):
    T = proj.shape[0]
    width = len(POOL_WINDOWS) * POOL_GROUP
    tm = _pick_tile(seq_len, tm, POOL_HALO)
    assert u_off % width == 0 and max(POOL_WINDOWS) <= POOL_HALO
    ublk = u_off // width
    rows_per_tile = tm // POOL_HALO
    return pl.pallas_call(
        functools.partial(_pool_kernel, tm=tm, tiles_per_seq=seq_len // tm),
        out_shape=jax.ShapeDtypeStruct((T, width), BF16),
        grid=(T // tm,),
        in_specs=[pl.BlockSpec((tm, width), lambda i: (i, ublk)),
                  pl.BlockSpec((POOL_HALO, width),
                               lambda i: (jnp.maximum(i * rows_per_tile - 1, 0), ublk)),
                  pl.BlockSpec(w_grp.shape, lambda i: (0, 0, 0)),
                  pl.BlockSpec((1, width), lambda i: (0, 0))],
        out_specs=pl.BlockSpec((tm, width), lambda i: (i, 0)),
        scratch_shapes=[pltpu.VMEM((tm + POOL_HALO, width), F32)],
        compiler_params=_params("parallel"),
        name="pool_mixer",
    )(proj, proj, w_grp, scale.reshape(1, width))


def _bias_tiles_kernel(tab_ref, o_ref, *, heads, blk):
    dist = pl.program_id(0)
    i = lax.broadcasted_iota(jnp.int32, (blk, blk), 0)
    j = lax.broadcasted_iota(jnp.int32, (blk, blk), 1)
    d = dist * blk + i - j
    n = jnp.maximum(d, 0)
    nf = jnp.maximum(n, 1).astype(F32)
    large = REL_MAX_EXACT + (jnp.log(nf / REL_MAX_EXACT) / math.log(REL_MAX_DIST / REL_MAX_EXACT)
                             * (REL_BUCKETS - REL_MAX_EXACT)).astype(jnp.int32)
    large = jnp.minimum(large, REL_BUCKETS - 1)
    bucket = jnp.where(n < REL_MAX_EXACT, n, large)
    masks = [bucket == b for b in range(REL_BUCKETS)]
    for h in range(heads):
        acc = jnp.zeros((blk, blk), F32)
        for b in range(REL_BUCKETS):
            acc = jnp.where(masks[b], tab_ref[h * REL_BUCKETS + b], acc)
        o_ref[h, 0] = jnp.where(d >= 0, acc, NEG)


def rel_bias_tiles(rel_table, nb, blk):
    heads = rel_table.shape[0]
    return pl.pallas_call(
        functools.partial(_bias_tiles_kernel, heads=heads, blk=blk),
        out_shape=jax.ShapeDtypeStruct((heads, nb, blk, blk), F32),
        grid=(nb,),
        in_specs=[pl.BlockSpec(memory_space=pltpu.SMEM)],
        out_specs=pl.BlockSpec((heads, 1, blk, blk), lambda d: (0, d, 0, 0)),
        compiler_params=_params("parallel"),
        name="rel_bias_tiles",
    )(rel_table.reshape(-1))


def _moba_kernel(q_ref, k_ref, v_ref, bias_ref, o_ref, kaug_ref, kbar_ref, *, nb, blk, topk, scale):
    qi = pl.program_id(2)
    dh = q_ref.shape[-1]
    seq = k_ref.shape[0]

    @pl.when(qi == 0)
    def _():
        kaug_ref[:, 0:dh] = k_ref[...]
        row_blk = lax.broadcasted_iota(jnp.int32, (seq, dh), 0) // blk
        col = lax.broadcasted_iota(jnp.int32, (seq, dh), 1)
        kaug_ref[:, dh:] = (row_blk == col).astype(BF16)
        kbar_ref[...] = jnp.zeros(kbar_ref.shape, F32)
        for n in range(nb):
            kblk = k_ref[n * blk:(n + 1) * blk, :].astype(F32)
            kbar_ref[n:n + 1, :] = jnp.mean(kblk, axis=0, keepdims=True)

    q = q_ref[...]
    kbar = kbar_ref[...]
    kbar_hi = kbar.astype(BF16)
    kbar_lo = (kbar - kbar_hi.astype(F32)).astype(BF16)
    gate = (lax.dot_general(q, kbar_hi, _NT, preferred_element_type=F32)
            + lax.dot_general(q, kbar_lo, _NT, preferred_element_type=F32))
    col = lax.broadcasted_iota(jnp.int32, gate.shape, 1)
    past = col < qi
    gate = jnp.where(past, gate, -jnp.inf)
    rank = jnp.zeros(gate.shape, jnp.int32)
    for m in range(nb - 1):
        gm = gate[:, m:m + 1]
        before = (gm > gate) | ((gm == gate) & (m < col))
        rank = rank + before.astype(jnp.int32)
    chosen = (past & (rank < topk)) | (col == qi)
    pen = jnp.where(chosen, 0.0, NEG).astype(BF16)
    q_aug = jnp.concatenate([q, pen], axis=1)

    def body(dist, carry):
        m_i, l_i, acc = carry
        start = pl.multiple_of((qi - dist) * blk, blk)
        kb = kaug_ref[pl.ds(start, blk), :]
        s = lax.dot_general(q_aug, kb, _NT, preferred_element_type=F32) * scale + bias_ref[dist]
        m_new = jnp.maximum(m_i, jnp.max(s, axis=1, keepdims=True))
        alpha = jnp.exp(m_i - m_new)
        p = jnp.exp(s - m_new)
        l_new = alpha * l_i + jnp.sum(p, axis=1, keepdims=True)
        vb = v_ref[pl.ds(start, blk), :]
        acc = alpha * acc + jnp.dot(p.astype(BF16), vb, preferred_element_type=F32)
        return m_new, l_new, acc

    init = (jnp.full((blk, 1), NEG, F32), jnp.zeros((blk, 1), F32), jnp.zeros((blk, dh), F32))
    _, l_i, acc = lax.fori_loop(0, qi + 1, body, init)
    o_ref[...] = (acc / l_i).astype(o_ref.dtype)


def moba_attention(proj3, q_off, k_off, v_off, bias_tiles, heads):
    B, S, _ = proj3.shape
    blk, dh = MOBA_BLOCK, HEAD_DIM
    assert S % blk == 0 and q_off % dh == 0 and k_off % dh == 0 and v_off % dh == 0
    nb = S // blk
    assert nb <= dh, "block one-hot columns must fit beside the keys"
    topk = min(MOBA_TOPK, nb)
    qb, kb_, vb_ = q_off // dh, k_off // dh, v_off // dh
    kern = functools.partial(_moba_kernel, nb=nb, blk=blk, topk=topk, scale=dh ** -0.5)
    return pl.pallas_call(
        kern,
        out_shape=jax.ShapeDtypeStruct((B, S, heads * dh), BF16),
        grid=(heads, B, nb),
        in_specs=[pl.BlockSpec((None, blk, dh), lambda h, b, i: (b, i, qb + h)),
                  pl.BlockSpec((None, S, dh), lambda h, b, i: (b, 0, kb_ + h)),
                  pl.BlockSpec((None, S, dh), lambda h, b, i: (b, 0, vb_ + h)),
                  pl.BlockSpec((None, nb, blk, blk), lambda h, b, i: (h, 0, 0, 0))],
        out_specs=pl.BlockSpec((None, blk, dh), lambda h, b, i: (b, i, h)),
        scratch_shapes=[pltpu.VMEM((S, 2 * dh), BF16), pltpu.VMEM((dh, dh), F32)],
        compiler_params=_params("parallel", "parallel", "arbitrary"),
        name="moba_attention",
    )(proj3, proj3, proj3, bias_tiles)


def _suffix_sum_matrix(bk):
    r = lax.broadcasted_iota(jnp.int32, (2 * bk, 2 * bk), 0) % bk
    c = lax.broadcasted_iota(jnp.int32, (2 * bk, 2 * bk), 1)
    return ((c >= bk) | (r > c)).astype(BF16)


def _sb_kernel(q_ref, k_ref, v_ref, o_ref, *, tq, bk, scale):
    qi = pl.program_id(2)
    dh = q_ref.shape[-1]
    q = q_ref[...]
    sums = _suffix_sum_matrix(bk)
    row = qi * tq + lax.broadcasted_iota(jnp.int32, (tq, bk), 0)
    col0 = lax.broadcasted_iota(jnp.int32, (tq, bk), 1)

    def block(j, run, acc, masked):
        start = pl.multiple_of(j * bk, bk)
        z = lax.dot_general(q, k_ref[pl.ds(start, bk), :], _NT, preferred_element_type=F32) * scale
        sp = jnp.maximum(z, 0.0) + jnp.log(1.0 + jnp.exp(-jnp.abs(z)))
        log_1m = -sp
        if masked:
            causal = (start + col0) < row
            log_1m = jnp.where(causal, log_1m, 0.0)
        hi = log_1m.astype(BF16)
        lo = (log_1m - hi.astype(F32)).astype(BF16)
        cs = jnp.dot(jnp.concatenate([hi, lo], axis=1), sums, preferred_element_type=F32)
        a = jnp.exp((z - sp) + (run + cs[:, :bk]))
        if masked:
            a = jnp.where(causal, a, 0.0)
        acc = acc + jnp.dot(a.astype(BF16), v_ref[pl.ds(start, bk), :], preferred_element_type=F32)
        return run + cs[:, bk:], acc

    run = jnp.zeros((tq, bk), F32)
    acc = jnp.zeros((tq, dh), F32)
    first = qi * (tq // bk)
    for d in reversed(range(tq // bk)):
        run, acc = block(first + d, run, acc, True)

    def body(t, carry):
        return block(first - 1 - t, carry[0], carry[1], False)

    run, acc = lax.fori_loop(0, first, body, (run, acc))
    o_ref[...] = acc.astype(o_ref.dtype)


def stick_breaking_attention(proj3, q_off, k_off, v_off, heads, *, tq=256, bk=128):
    B, S, _ = proj3.shape
    dh = HEAD_DIM
    assert S % tq == 0 and tq % bk == 0 and q_off % dh == 0 and k_off % dh == 0 and v_off % dh == 0
    qb, kb_, vb_ = q_off // dh, k_off // dh, v_off // dh
    return pl.pallas_call(
        functools.partial(_sb_kernel, tq=tq, bk=bk, scale=dh ** -0.5),
        out_shape=jax.ShapeDtypeStruct((B, S, heads * dh), BF16),
        grid=(heads, B, S // tq),
        in_specs=[pl.BlockSpec((None, tq, dh), lambda h, b, i: (b, i, qb + h)),
                  pl.BlockSpec((None, S, dh), lambda h, b, i: (b, 0, kb_ + h)),
                  pl.BlockSpec((None, S, dh), lambda h, b, i: (b, 0, vb_ + h))],
        out_specs=pl.BlockSpec((None, tq, dh), lambda h, b, i: (b, i, h)),
        compiler_params=_params("parallel", "parallel", "parallel"),
        name="stick_breaking_attention",
    )(proj3, proj3, proj3)


def kernel(x, norm_mix, norm_ffn, w_in, w_pool, pool_scale, w_br_a, w_br_b, w_br_c, w_out, w_gate,
           w_up, w_down, rel_bias, norm_final):
    B, S, D = x.shape
    depth = w_in.shape[0]
    T = B * S
    moba_w = MOBA_HEADS * HEAD_DIM
    sb_w = SB_HEADS * HEAD_DIM
    pool_w = len(POOL_WINDOWS) * POOL_GROUP
    off_qa, off_ka, off_va = 0, moba_w, 2 * moba_w
    off_u = 3 * moba_w
    off_qs = off_u + pool_w
    off_ks, off_vs = off_qs + sb_w, off_qs + 2 * sb_w
    off_gates = off_qs + 3 * sb_w

    bias_tiles = rel_bias_tiles(rel_bias, S // MOBA_BLOCK, MOBA_BLOCK)
    h = x.reshape(T, D)
    for l in range(depth):
        proj = norm_matmul(h, norm_mix[l], w_in[l].astype(BF16))
        proj3 = proj.reshape(B, S, proj.shape[1])
        ya = moba_attention(proj3, off_qa, off_ka, off_va, bias_tiles, MOBA_HEADS).reshape(T, moba_w)
        yb = pool_mixer(proj, off_u, w_pool[l].astype(BF16), pool_scale[l], S)
        yc = stick_breaking_attention(proj3, off_qs, off_ks, off_vs, SB_HEADS).reshape(T, sb_w)
        m = merge_branches(ya, yb, yc, proj, off_gates, w_br_a[l].astype(BF16),
                           w_br_b[l].astype(BF16), w_br_c[l].astype(BF16))
        h = matmul_res(m, w_out[l].astype(BF16), h)
        act = ffn_up(h, norm_ffn[l], w_gate[l].astype(BF16), w_up[l].astype(BF16))
        h = matmul_res(act, w_down[l].astype(BF16), h)
    return final_norm(h, norm_final).reshape(B, S, D)
```

```python
import functools
import math

import jax
import jax.numpy as jnp
from jax import lax
from jax.experimental import pallas as pl
from jax.experimental.pallas import tpu as pltpu

F32 = jnp.float32
BF16 = jnp.bfloat16

HEAD_DIM = 128
MOBA_HEADS = 8
MOBA_BLOCK = 256
MOBA_TOPK = 3
MOBA_QTILE = 512
SB_HEADS = 8
POOL_WINDOWS = (2, 4, 8, 16)
POOL_GROUP = 256
POOL_HALO = 16
REL_BUCKETS = 32
REL_MAX_EXACT = 16
REL_MAX_DIST = 2048
EPS = 1e-6
NEG = -1e30
EXP_ZERO_BELOW = -105.0

V7X_LANES = 128
V7X_VMEM_LIMIT_BYTES = 56 * 1024 * 1024

_NT = (((1,), (1,)), ((), ()))


def _params(*semantics):
    return pltpu.CompilerParams(dimension_semantics=semantics,
                                vmem_limit_bytes=V7X_VMEM_LIMIT_BYTES)


def _pick_tile(n, want, align):
    t = min(want, n)
    while t > align and (n % t or t % align):
        t -= align
    assert n % t == 0 and t % align == 0, (n, want, align)
    return t


def _rms_normalize(x, g):
    ms = jnp.mean(x * x, axis=-1, keepdims=True)
    return x * lax.rsqrt(ms + EPS) * g


def _norm_matmul_kernel(x_ref, g_ref, w_ref, o_ref, xn_ref):
    @pl.when(pl.program_id(1) == 0)
    def _():
        xn_ref[...] = _rms_normalize(x_ref[...], g_ref[...]).astype(BF16)

    o_ref[...] = jnp.dot(xn_ref[...], w_ref[...], preferred_element_type=F32).astype(o_ref.dtype)


def norm_matmul(x, g, w, *, out_dtype=BF16, tm=1024, tn=1024):
    T, K = x.shape
    N = w.shape[1]
    tm = _pick_tile(T, tm, 16)
    tn = _pick_tile(N, tn, V7X_LANES)
    return pl.pallas_call(
        _norm_matmul_kernel,
        out_shape=jax.ShapeDtypeStruct((T, N), out_dtype),
        grid=(T // tm, N // tn),
        in_specs=[pl.BlockSpec((tm, K), lambda i, j: (i, 0)),
                  pl.BlockSpec((1, K), lambda i, j: (0, 0)),
                  pl.BlockSpec((K, tn), lambda i, j: (0, j))],
        out_specs=pl.BlockSpec((tm, tn), lambda i, j: (i, j)),
        scratch_shapes=[pltpu.VMEM((tm, K), BF16)],
        compiler_params=_params("parallel", "arbitrary"),
        name="norm_matmul",
    )(x, g.reshape(1, K), w)


def _ffn_up_kernel(x_ref, g_ref, wg_ref, wu_ref, o_ref, xn_ref):
    @pl.when(pl.program_id(1) == 0)
    def _():
        xn_ref[...] = _rms_normalize(x_ref[...], g_ref[...]).astype(BF16)

    xn = xn_ref[...]
    a = jnp.dot(xn, wg_ref[...], preferred_element_type=F32)
    b = jnp.dot(xn, wu_ref[...], preferred_element_type=F32)
    o_ref[...] = (a * jax.nn.sigmoid(a) * b).astype(o_ref.dtype)


def ffn_up(x, g, wg, wu, *, tm=1024, tn=512):
    T, K = x.shape
    N = wg.shape[1]
    tm = _pick_tile(T, tm, 16)
    tn = _pick_tile(N, tn, V7X_LANES)
    return pl.pallas_call(
        _ffn_up_kernel,
        out_shape=jax.ShapeDtypeStruct((T, N), BF16),
        grid=(T // tm, N // tn),
        in_specs=[pl.BlockSpec((tm, K), lambda i, j: (i, 0)),
                  pl.BlockSpec((1, K), lambda i, j: (0, 0)),
                  pl.BlockSpec((K, tn), lambda i, j: (0, j)),
                  pl.BlockSpec((K, tn), lambda i, j: (0, j))],
        out_specs=pl.BlockSpec((tm, tn), lambda i, j: (i, j)),
        scratch_shapes=[pltpu.VMEM((tm, K), BF16)],
        compiler_params=_params("parallel", "arbitrary"),
        name="ffn_up",
    )(x, g.reshape(1, K), wg, wu)


def _matmul_res_kernel(x_ref, w_ref, r_ref, o_ref):
    o_ref[...] = r_ref[...] + jnp.dot(x_ref[...], w_ref[...], preferred_element_type=F32)


def matmul_res(x, w, res, *, tm=1024, tn=512):
    T, K = x.shape
    N = w.shape[1]
    tm = _pick_tile(T, tm, 16)
    tn = _pick_tile(N, tn, V7X_LANES)
    return pl.pallas_call(
        _matmul_res_kernel,
        out_shape=jax.ShapeDtypeStruct((T, N), F32),
        grid=(T // tm, N // tn),
        in_specs=[pl.BlockSpec((tm, K), lambda i, j: (i, 0)),
                  pl.BlockSpec((K, tn), lambda i, j: (0, j)),
                  pl.BlockSpec((tm, tn), lambda i, j: (i, j))],
        out_specs=pl.BlockSpec((tm, tn), lambda i, j: (i, j)),
        compiler_params=_params("parallel", "parallel"),
        name="matmul_res",
    )(x, w, res)


def _merge_kernel(ya_ref, yb_ref, yc_ref, ga_ref, gb_ref, gc_ref, wa_ref, wb_ref, wc_ref, o_ref):
    def branch(y_ref, g_ref, w_ref):
        gate = jax.nn.sigmoid(g_ref[...].astype(F32))
        return gate * jnp.dot(y_ref[...], w_ref[...], preferred_element_type=F32)

    m = branch(ya_ref, ga_ref, wa_ref) + branch(yb_ref, gb_ref, wb_ref) + branch(yc_ref, gc_ref, wc_ref)
    o_ref[...] = m.astype(o_ref.dtype)


def merge_branches(ya, yb, yc, proj, gate_off, wa, wb, wc, *, tm=1024, tn=512):
    T, K = ya.shape
    N = wa.shape[1]
    tm = _pick_tile(T, tm, 16)
    tn = _pick_tile(N, tn, V7X_LANES)
    assert gate_off % tn == 0
    y_spec = pl.BlockSpec((tm, K), lambda i, j: (i, 0))
    w_spec = pl.BlockSpec((K, tn), lambda i, j: (0, j))

    def gate_spec(branch):
        base = (gate_off + branch * N) // tn
        return pl.BlockSpec((tm, tn), lambda i, j: (i, base + j))

    return pl.pallas_call(
        _merge_kernel,
        out_shape=jax.ShapeDtypeStruct((T, N), BF16),
        grid=(T // tm, N // tn),
        in_specs=[y_spec, y_spec, y_spec, gate_spec(0), gate_spec(1), gate_spec(2),
                  w_spec, w_spec, w_spec],
        out_specs=pl.BlockSpec((tm, tn), lambda i, j: (i, j)),
        compiler_params=_params("parallel", "parallel"),
        name="merge_branches",
    )(ya, yb, yc, proj, proj, proj, wa, wb, wc)


def _final_norm_kernel(x_ref, g_ref, o_ref):
    o_ref[...] = _rms_normalize(x_ref[...], g_ref[...])


def final_norm(x, g, *, tm=512):
    T, K = x.shape
    tm = _pick_tile(T, tm, 8)
    return pl.pallas_call(
        _final_norm_kernel,
        out_shape=jax.ShapeDtypeStruct((T, K), F32),
        grid=(T // tm,),
        in_specs=[pl.BlockSpec((tm, K), lambda i: (i, 0)),
                  pl.BlockSpec((1, K), lambda i: (0, 0))],
        out_specs=pl.BlockSpec((tm, K), lambda i: (i, 0)),
        compiler_params=_params("parallel"),
        name="final_norm",
    )(x, g.reshape(1, K))


def _pool_kernel(u_ref, halo_ref, w_ref, sc_ref, o_ref, ext_ref, *, tm, tiles_per_seq):
    t_in_seq = pl.program_id(0) % tiles_per_seq
    halo = halo_ref[...].astype(F32)
    ext_ref[0:POOL_HALO, :] = jnp.where(t_in_seq == 0, 0.0, halo)
    ext_ref[POOL_HALO:, :] = u_ref[...].astype(F32)
    pos = t_in_seq * tm + lax.broadcasted_iota(jnp.int32, (tm, 1), 0)
    for gi, win in enumerate(POOL_WINDOWS):
        cols = slice(gi * POOL_GROUP, (gi + 1) * POOL_GROUP)
        x = ext_ref[POOL_HALO:, cols]
        total = x
        for back in range(1, win):
            total = total + ext_ref[pl.ds(POOL_HALO - back, tm), cols]
        cnt = jnp.minimum(pos + 1, win).astype(F32)
        pooled = total / cnt - x
        y = jnp.dot(pooled.astype(BF16), w_ref[gi], preferred_element_type=F32)
        o_ref[:, cols] = (y * sc_ref[:, cols]).astype(o_ref.dtype)


def pool_mixer(proj, u_off, w_grp, scale, seq_len, *, tm=512):
    T = proj.shape[0]
    width = len(POOL_WINDOWS) * POOL_GROUP
    tm = _pick_tile(seq_len, tm, POOL_HALO)
    assert u_off % width == 0 and max(POOL_WINDOWS) <= POOL_HALO
    ublk = u_off // width
    rows_per_tile = tm // POOL_HALO
    return pl.pallas_call(
        functools.partial(_pool_kernel, tm=tm, tiles_per_seq=seq_len // tm),
        out_shape=jax.ShapeDtypeStruct((T, width), BF16),
        grid=(T // tm,),
        in_specs=[pl.BlockSpec((tm, width), lambda i: (i, ublk)),
                  pl.BlockSpec((POOL_HALO, width),
                               lambda i: (jnp.maximum(i * rows_per_tile - 1, 0), ublk)),
                  pl.BlockSpec(w_grp.shape, lambda i: (0, 0, 0)),
                  pl.BlockSpec((1, width), lambda i: (0, 0))],
        out_specs=pl.BlockSpec((tm, width), lambda i: (i, 0)),
        scratch_shapes=[pltpu.VMEM((tm + POOL_HALO, width), F32)],
        compiler_params=_params("parallel"),
        name="pool_mixer",
    )(proj, proj, w_grp, scale.reshape(1, width))


def _bias_tiles_kernel(tab_ref, o_ref, *, heads, blk):
    dist = pl.program_id(0)
    i = lax.broadcasted_iota(jnp.int32, (blk, blk), 0)
    j = lax.broadcasted_iota(jnp.int32, (blk, blk), 1)
    d = dist * blk + i - j
    n = jnp.maximum(d, 0)
    nf = jnp.maximum(n, 1).astype(F32)
    large = REL_MAX_EXACT + (jnp.log(nf / REL_MAX_EXACT) / math.log(REL_MAX_DIST / REL_MAX_EXACT)
                             * (REL_BUCKETS - REL_MAX_EXACT)).astype(jnp.int32)
    large = jnp.minimum(large, REL_BUCKETS - 1)
    bucket = jnp.where(n < REL_MAX_EXACT, n, large)
    masks = [bucket == b for b in range(REL_BUCKETS)]
    for h in range(heads):
        acc = jnp.zeros((blk, blk), F32)
        for b in range(REL_BUCKETS):
            acc = jnp.where(masks[b], tab_ref[h * REL_BUCKETS + b], acc)
        o_ref[h, 0] = jnp.where(d >= 0, acc, NEG)


def rel_bias_tiles(rel_table, nb, blk):
    heads = rel_table.shape[0]
    return pl.pallas_call(
        functools.partial(_bias_tiles_kernel, heads=heads, blk=blk),
        out_shape=jax.ShapeDtypeStruct((heads, nb, blk, blk), F32),
        grid=(nb,),
        in_specs=[pl.BlockSpec(memory_space=pltpu.SMEM)],
        out_specs=pl.BlockSpec((heads, 1, blk, blk), lambda d: (0, d, 0, 0)),
        compiler_params=_params("parallel"),
        name="rel_bias_tiles",
    )(rel_table.reshape(-1))


def _moba_kernel(q_ref, k_ref, v_ref, bias_ref, o_ref, kaug_ref, kbar_ref, *, nb, blk, topk, scale):
    qi = pl.program_id(2)
    tq, dh = q_ref.shape
    seq = k_ref.shape[0]
    nbp = kbar_ref.shape[0]

    @pl.when(qi == 0)
    def _():
        kaug_ref[:, 0:dh] = k_ref[...]
        row_blk = lax.broadcasted_iota(jnp.int32, (seq, dh), 0) // blk
        col = lax.broadcasted_iota(jnp.int32, (seq, dh), 1)
        kaug_ref[:, dh:] = (row_blk == col).astype(BF16)
        kbar_ref[...] = jnp.zeros(kbar_ref.shape, F32)
        for n in range(nb):
            kblk = k_ref[n * blk:(n + 1) * blk, :].astype(F32)
            kbar_ref[n:n + 1, :] = jnp.mean(kblk, axis=0, keepdims=True)

    q = q_ref[...]
    kbar = kbar_ref[...]
    kbar_hi = kbar.astype(BF16)
    kbar_lo = (kbar - kbar_hi.astype(F32)).astype(BF16)
    gate = (lax.dot_general(kbar_hi, q, _NT, preferred_element_type=F32)
            + lax.dot_general(kbar_lo, q, _NT, preferred_element_type=F32))
    blk_id = lax.broadcasted_iota(jnp.int32, (nbp, tq), 0)
    own = qi * (tq // blk) + lax.broadcasted_iota(jnp.int32, (nbp, tq), 1) // blk
    past = blk_id < own
    gate = jnp.where(past, gate, -jnp.inf)
    rank = jnp.zeros((nbp, tq), jnp.int32)
    for m in range(nb - 1):
        gm = gate[m:m + 1, :]
        before = (gm > gate) | ((gm == gate) & (m < blk_id))
        rank = rank + before.astype(jnp.int32)
    chosen = (past & (rank < topk)) | (blk_id == own)
    pen_t = jnp.where(chosen, 0.0, NEG)
    pen_t = jnp.concatenate([pen_t, jnp.zeros((dh - nbp, tq), F32)], axis=0)
    q_aug = jnp.concatenate([q, pen_t.T.astype(BF16)], axis=1)

    def body(dist, carry):
        m_i, l_i, acc = carry
        start = pl.multiple_of((qi - dist) * tq, tq)
        kb = kaug_ref[pl.ds(start, tq), :]
        s = lax.dot_general(q_aug, kb, _NT, preferred_element_type=F32) * scale + bias_ref[dist]
        m_new = jnp.maximum(m_i, jnp.max(s, axis=1, keepdims=True))
        alpha = jnp.exp(m_i - m_new)
        p = jnp.exp(s - m_new)
        l_new = alpha * l_i + jnp.sum(p, axis=1, keepdims=True)
        vb = v_ref[pl.ds(start, tq), :]
        acc = alpha * acc + jnp.dot(p.astype(BF16), vb, preferred_element_type=F32)
        return m_new, l_new, acc

    init = (jnp.full((tq, 1), NEG, F32), jnp.zeros((tq, 1), F32), jnp.zeros((tq, dh), F32))
    _, l_i, acc = lax.fori_loop(0, qi + 1, body, init)
    o_ref[...] = (acc / l_i).astype(o_ref.dtype)


def moba_attention(proj3, q_off, k_off, v_off, bias_tiles, heads):
    B, S, _ = proj3.shape
    blk, dh = MOBA_BLOCK, HEAD_DIM
    tq = bias_tiles.shape[-1]
    assert S % tq == 0 and tq % blk == 0 and q_off % dh == 0 and k_off % dh == 0 and v_off % dh == 0
    nb = S // blk
    nbp = -(-nb // 8) * 8
    assert nbp <= dh, "block one-hot columns must fit beside the keys"
    topk = min(MOBA_TOPK, nb)
    qb, kb_, vb_ = q_off // dh, k_off // dh, v_off // dh
    kern = functools.partial(_moba_kernel, nb=nb, blk=blk, topk=topk, scale=dh ** -0.5)
    return pl.pallas_call(
        kern,
        out_shape=jax.ShapeDtypeStruct((B, S, heads * dh), BF16),
        grid=(heads, B, S // tq),
        in_specs=[pl.BlockSpec((None, tq, dh), lambda h, b, i: (b, i, qb + h)),
                  pl.BlockSpec((None, S, dh), lambda h, b, i: (b, 0, kb_ + h)),
                  pl.BlockSpec((None, S, dh), lambda h, b, i: (b, 0, vb_ + h)),
                  pl.BlockSpec((None, S // tq, tq, tq), lambda h, b, i: (h, 0, 0, 0))],
        out_specs=pl.BlockSpec((None, tq, dh), lambda h, b, i: (b, i, h)),
        scratch_shapes=[pltpu.VMEM((S, 2 * dh), BF16), pltpu.VMEM((nbp, dh), F32)],
        compiler_params=_params("parallel", "parallel", "arbitrary"),
        name="moba_attention",
    )(proj3, proj3, proj3, bias_tiles)


def _suffix_sum_matrix(bk):
    r = lax.broadcasted_iota(jnp.int32, (2 * bk, 2 * bk), 0) % bk
    c = lax.broadcasted_iota(jnp.int32, (2 * bk, 2 * bk), 1)
    return ((c >= bk) | (r > c)).astype(BF16)


def _sb_kernel(q_ref, k_ref, v_ref, o_ref, *, scale):
    qi = pl.program_id(2)
    tq, dh = q_ref.shape
    half = tq // 2
    q = q_ref[...]
    sums = _suffix_sum_matrix(half)

    def tile(j, run, acc, valid):
        start = pl.multiple_of(j * tq, tq)
        z = lax.dot_general(q, k_ref[pl.ds(start, tq), :], _NT, preferred_element_type=F32) * scale
        sp = jnp.maximum(z, 0.0) + jnp.log(1.0 + jnp.exp(-jnp.abs(z)))
        log_1m = -sp
        if valid is not None:
            log_1m = jnp.where(valid, log_1m, 0.0)
        hi = log_1m.astype(BF16)
        lo = (log_1m - hi.astype(F32)).astype(BF16)
        cs_b = jnp.dot(jnp.concatenate([hi[:, half:], lo[:, half:]], axis=1), sums,
                       preferred_element_type=F32)
        cs_a = jnp.dot(jnp.concatenate([hi[:, :half], lo[:, :half]], axis=1), sums,
                       preferred_element_type=F32)
        run_b = run + cs_b[:, half:]
        after = jnp.concatenate([run_b + cs_a[:, :half], run + cs_b[:, :half]], axis=1)
        a = jnp.exp((z - sp) + after)
        if valid is not None:
            a = jnp.where(valid, a, 0.0)
        acc = acc + jnp.dot(a.astype(BF16), v_ref[pl.ds(start, tq), :], preferred_element_type=F32)
        return run_b + cs_a[:, half:], acc

    run = jnp.zeros((tq, half), F32)
    acc = jnp.zeros((tq, dh), F32)
    row = lax.broadcasted_iota(jnp.int32, (tq, tq), 0)
    col = lax.broadcasted_iota(jnp.int32, (tq, tq), 1)
    run, acc = tile(qi, run, acc, col < row)
    run, acc = tile(jnp.maximum(qi - 1, 0), run, acc, qi > 0)

    def cond(carry):
        j, _, _, run_max = carry
        return (j >= 0) & (run_max > EXP_ZERO_BELOW)

    def body(carry):
        j, run, acc, _ = carry
        run, acc = tile(j, run, acc, None)
        return j - 1, run, acc, jnp.max(run)

    _, _, acc, _ = lax.while_loop(cond, body, (qi - 2, run, acc, jnp.max(run)))
    o_ref[...] = acc.astype(o_ref.dtype)


def stick_breaking_attention(proj3, q_off, k_off, v_off, heads, *, tq=2 * V7X_LANES):
    B, S, _ = proj3.shape
    dh = HEAD_DIM
    assert tq == 2 * V7X_LANES, "suffix sums work on 128-key halves of a tile"
    assert S % tq == 0 and q_off % dh == 0 and k_off % dh == 0 and v_off % dh == 0
    qb, kb_, vb_ = q_off // dh, k_off // dh, v_off // dh
    return pl.pallas_call(
        functools.partial(_sb_kernel, scale=dh ** -0.5),
        out_shape=jax.ShapeDtypeStruct((B, S, heads * dh), BF16),
        grid=(heads, B, S // tq),
        in_specs=[pl.BlockSpec((None, tq, dh), lambda h, b, i: (b, i, qb + h)),
                  pl.BlockSpec((None, S, dh), lambda h, b, i: (b, 0, kb_ + h)),
                  pl.BlockSpec((None, S, dh), lambda h, b, i: (b, 0, vb_ + h))],
        out_specs=pl.BlockSpec((None, tq, dh), lambda h, b, i: (b, i, h)),
        compiler_params=_params("parallel", "parallel", "parallel"),
        name="stick_breaking_attention",
    )(proj3, proj3, proj3)


def kernel(x, norm_mix, norm_ffn, w_in, w_pool, pool_scale, w_br_a, w_br_b, w_br_c, w_out, w_gate,
           w_up, w_down, rel_bias, norm_final):
    B, S, D = x.shape
    depth = w_in.shape[0]
    T = B * S
    moba_w = MOBA_HEADS * HEAD_DIM
    sb_w = SB_HEADS * HEAD_DIM
    pool_w = len(POOL_WINDOWS) * POOL_GROUP
    off_qa, off_ka, off_va = 0, moba_w, 2 * moba_w
    off_u = 3 * moba_w
    off_qs = off_u + pool_w
    off_ks, off_vs = off_qs + sb_w, off_qs + 2 * sb_w
    off_gates = off_qs + 3 * sb_w

    bias_tiles = rel_bias_tiles(rel_bias, S // MOBA_QTILE, MOBA_QTILE)
    h = x.reshape(T, D)
    for l in range(depth):
        proj = norm_matmul(h, norm_mix[l], w_in[l].astype(BF16))
        proj3 = proj.reshape(B, S, proj.shape[1])
        ya = moba_attention(proj3, off_qa, off_ka, off_va, bias_tiles, MOBA_HEADS).reshape(T, moba_w)
        yb = pool_mixer(proj, off_u, w_pool[l].astype(BF16), pool_scale[l], S)
        yc = stick_breaking_attention(proj3, off_qs, off_ks, off_vs, SB_HEADS).reshape(T, sb_w)
        m = merge_branches(ya, yb, yc, proj, off_gates, w_br_a[l].astype(BF16),
                           w_br_b[l].astype(BF16), w_br_c[l].astype(BF16))
        h = matmul_res(m, w_out[l].astype(BF16), h)
        act = ffn_up(h, norm_ffn[l], w_gate[l].astype(BF16), w_up[l].astype(BF16))
        h = matmul_res(act, w_down[l].astype(BF16), h)
    return final_norm(h, norm_final).reshape(B, S, D)
```

```python
import functools
import math

import jax
import jax.numpy as jnp
from jax import lax
from jax.experimental import pallas as pl
from jax.experimental.pallas import tpu as pltpu

F32 = jnp.float32
BF16 = jnp.bfloat16

HEAD_DIM = 128
MOBA_HEADS = 8
MOBA_BLOCK = 256
MOBA_TOPK = 3
MOBA_QTILE = 512
SB_HEADS = 8
POOL_WINDOWS = (2, 4, 8, 16)
POOL_GROUP = 256
POOL_HALO = 16
REL_BUCKETS = 32
REL_MAX_EXACT = 16
REL_MAX_DIST = 2048
EPS = 1e-6
NEG = -1e30
LOG2E = math.log2(math.e)
EXP2_ZERO_BELOW = -152.0

V7X_LANES = 128
V7X_VMEM_LIMIT_BYTES = 56 * 1024 * 1024

_NT = (((1,), (1,)), ((), ()))


def _params(*semantics):
    return pltpu.CompilerParams(dimension_semantics=semantics,
                                vmem_limit_bytes=V7X_VMEM_LIMIT_BYTES)


def _pick_tile(n, want, align):
    t = min(want, n)
    while t > align and (n % t or t % align):
        t -= align
    assert n % t == 0 and t % align == 0, (n, want, align)
    return t


def _rms_normalize(x, g):
    ms = jnp.mean(x * x, axis=-1, keepdims=True)
    return x * lax.rsqrt(ms + EPS) * g


def _norm_matmul_kernel(x_ref, g_ref, w_ref, o_ref, xn_ref):
    @pl.when(pl.program_id(1) == 0)
    def _():
        xn_ref[...] = _rms_normalize(x_ref[...], g_ref[...]).astype(BF16)

    o_ref[...] = jnp.dot(xn_ref[...], w_ref[...], preferred_element_type=F32).astype(o_ref.dtype)


def norm_matmul(x, g, w, layer, *, out_dtype=BF16, tm=1024, tn=1024):
    T, K = x.shape
    N = w.shape[2]
    tm = _pick_tile(T, tm, 16)
    tn = _pick_tile(N, tn, V7X_LANES)
    return pl.pallas_call(
        _norm_matmul_kernel,
        out_shape=jax.ShapeDtypeStruct((T, N), out_dtype),
        grid=(T // tm, N // tn),
        in_specs=[pl.BlockSpec((tm, K), lambda i, j: (i, 0)),
                  pl.BlockSpec((1, K), lambda i, j: (0, 0)),
                  pl.BlockSpec((None, K, tn), lambda i, j: (layer, 0, j))],
        out_specs=pl.BlockSpec((tm, tn), lambda i, j: (i, j)),
        scratch_shapes=[pltpu.VMEM((tm, K), BF16)],
        compiler_params=_params("parallel", "arbitrary"),
        name="norm_matmul",
    )(x, g.reshape(1, K), w)


def _ffn_up_kernel(x_ref, g_ref, wg_ref, wu_ref, o_ref, xn_ref):
    @pl.when(pl.program_id(1) == 0)
    def _():
        xn_ref[...] = _rms_normalize(x_ref[...], g_ref[...]).astype(BF16)

    xn = xn_ref[...]
    a = jnp.dot(xn, wg_ref[...], preferred_element_type=F32)
    b = jnp.dot(xn, wu_ref[...], preferred_element_type=F32)
    o_ref[...] = (a * jax.nn.sigmoid(a) * b).astype(o_ref.dtype)


def ffn_up(x, g, wg, wu, layer, *, tm=1024, tn=512):
    T, K = x.shape
    N = wg.shape[2]
    tm = _pick_tile(T, tm, 16)
    tn = _pick_tile(N, tn, V7X_LANES)
    return pl.pallas_call(
        _ffn_up_kernel,
        out_shape=jax.ShapeDtypeStruct((T, N), BF16),
        grid=(T // tm, N // tn),
        in_specs=[pl.BlockSpec((tm, K), lambda i, j: (i, 0)),
                  pl.BlockSpec((1, K), lambda i, j: (0, 0)),
                  pl.BlockSpec((None, K, tn), lambda i, j: (layer, 0, j)),
                  pl.BlockSpec((None, K, tn), lambda i, j: (layer, 0, j))],
        out_specs=pl.BlockSpec((tm, tn), lambda i, j: (i, j)),
        scratch_shapes=[pltpu.VMEM((tm, K), BF16)],
        compiler_params=_params("parallel", "arbitrary"),
        name="ffn_up",
    )(x, g.reshape(1, K), wg, wu)


def _matmul_res_kernel(x_ref, w_ref, r_ref, o_ref):
    o_ref[...] = r_ref[...] + jnp.dot(x_ref[...], w_ref[...], preferred_element_type=F32)


def matmul_res(x, w, layer, res, *, tm=1024, tn=512):
    T, K = x.shape
    N = w.shape[2]
    tm = _pick_tile(T, tm, 16)
    tn = _pick_tile(N, tn, V7X_LANES)
    return pl.pallas_call(
        _matmul_res_kernel,
        out_shape=jax.ShapeDtypeStruct((T, N), F32),
        grid=(T // tm, N // tn),
        in_specs=[pl.BlockSpec((tm, K), lambda i, j: (i, 0)),
                  pl.BlockSpec((None, K, tn), lambda i, j: (layer, 0, j)),
                  pl.BlockSpec((tm, tn), lambda i, j: (i, j))],
        out_specs=pl.BlockSpec((tm, tn), lambda i, j: (i, j)),
        compiler_params=_params("parallel", "parallel"),
        name="matmul_res",
    )(x, w, res)


def _merge_kernel(ya_ref, yb_ref, yc_ref, ga_ref, gb_ref, gc_ref, wa_ref, wb_ref, wc_ref, o_ref):
    def branch(y_ref, g_ref, w_ref):
        gate = jax.nn.sigmoid(g_ref[...].astype(F32))
        return gate * jnp.dot(y_ref[...], w_ref[...], preferred_element_type=F32)

    m = branch(ya_ref, ga_ref, wa_ref) + branch(yb_ref, gb_ref, wb_ref) + branch(yc_ref, gc_ref, wc_ref)
    o_ref[...] = m.astype(o_ref.dtype)


def merge_branches(ya, yb, yc, proj, gate_off, wa, wb, wc, layer, *, tm=1024, tn=512):
    T, K = ya.shape
    N = wa.shape[2]
    tm = _pick_tile(T, tm, 16)
    tn = _pick_tile(N, tn, V7X_LANES)
    assert gate_off % tn == 0
    y_spec = pl.BlockSpec((tm, K), lambda i, j: (i, 0))
    w_spec = pl.BlockSpec((None, K, tn), lambda i, j: (layer, 0, j))

    def gate_spec(branch):
        base = (gate_off + branch * N) // tn
        return pl.BlockSpec((tm, tn), lambda i, j: (i, base + j))

    return pl.pallas_call(
        _merge_kernel,
        out_shape=jax.ShapeDtypeStruct((T, N), BF16),
        grid=(T // tm, N // tn),
        in_specs=[y_spec, y_spec, y_spec, gate_spec(0), gate_spec(1), gate_spec(2),
                  w_spec, w_spec, w_spec],
        out_specs=pl.BlockSpec((tm, tn), lambda i, j: (i, j)),
        compiler_params=_params("parallel", "parallel"),
        name="merge_branches",
    )(ya, yb, yc, proj, proj, proj, wa, wb, wc)


def _final_norm_kernel(x_ref, g_ref, o_ref):
    o_ref[...] = _rms_normalize(x_ref[...], g_ref[...])


def final_norm(x, g, *, tm=512):
    T, K = x.shape
    tm = _pick_tile(T, tm, 8)
    return pl.pallas_call(
        _final_norm_kernel,
        out_shape=jax.ShapeDtypeStruct((T, K), F32),
        grid=(T // tm,),
        in_specs=[pl.BlockSpec((tm, K), lambda i: (i, 0)),
                  pl.BlockSpec((1, K), lambda i: (0, 0))],
        out_specs=pl.BlockSpec((tm, K), lambda i: (i, 0)),
        compiler_params=_params("parallel"),
        name="final_norm",
    )(x, g.reshape(1, K))


def _pool_kernel(u_ref, halo_ref, w_ref, sc_ref, o_ref, ext_ref, *, tm, tiles_per_seq):
    t_in_seq = pl.program_id(0) % tiles_per_seq
    halo = halo_ref[...].astype(F32)
    ext_ref[0:POOL_HALO, :] = jnp.where(t_in_seq == 0, 0.0, halo)
    ext_ref[POOL_HALO:, :] = u_ref[...].astype(F32)
    pos = t_in_seq * tm + lax.broadcasted_iota(jnp.int32, (tm, 1), 0)
    for gi, win in enumerate(POOL_WINDOWS):
        cols = slice(gi * POOL_GROUP, (gi + 1) * POOL_GROUP)
        x = ext_ref[POOL_HALO:, cols]
        total = x
        for back in range(1, win):
            total = total + ext_ref[pl.ds(POOL_HALO - back, tm), cols]
        cnt = jnp.minimum(pos + 1, win).astype(F32)
        pooled = total / cnt - x
        y = jnp.dot(pooled.astype(BF16), w_ref[gi], preferred_element_type=F32)
        o_ref[:, cols] = (y * sc_ref[:, cols]).astype(o_ref.dtype)


def pool_mixer(proj, u_off, w_grp, layer, scale, seq_len, *, tm=512):
    T = proj.shape[0]
    width = len(POOL_WINDOWS) * POOL_GROUP
    tm = _pick_tile(seq_len, tm, POOL_HALO)
    assert u_off % width == 0 and max(POOL_WINDOWS) <= POOL_HALO
    ublk = u_off // width
    rows_per_tile = tm // POOL_HALO
    return pl.pallas_call(
        functools.partial(_pool_kernel, tm=tm, tiles_per_seq=seq_len // tm),
        out_shape=jax.ShapeDtypeStruct((T, width), BF16),
        grid=(T // tm,),
        in_specs=[pl.BlockSpec((tm, width), lambda i: (i, ublk)),
                  pl.BlockSpec((POOL_HALO, width),
                               lambda i: (jnp.maximum(i * rows_per_tile - 1, 0), ublk)),
                  pl.BlockSpec((None,) + w_grp.shape[1:], lambda i: (layer, 0, 0, 0)),
                  pl.BlockSpec((1, width), lambda i: (0, 0))],
        out_specs=pl.BlockSpec((tm, width), lambda i: (i, 0)),
        scratch_shapes=[pltpu.VMEM((tm + POOL_HALO, width), F32)],
        compiler_params=_params("parallel"),
        name="pool_mixer",
    )(proj, proj, w_grp, scale.reshape(1, width))


def _bucket_range(d_lo, d_hi):
    def bucket(d):
        if d < REL_MAX_EXACT:
            return d
        big = REL_MAX_EXACT + int(math.log(d / REL_MAX_EXACT) / math.log(REL_MAX_DIST / REL_MAX_EXACT)
                                  * (REL_BUCKETS - REL_MAX_EXACT))
        return min(big, REL_BUCKETS - 1)
    return max(bucket(max(d_lo, 0)) - 1, 0), min(bucket(d_hi) + 1, REL_BUCKETS - 1)


def _bias_tiles_kernel(tab_ref, o_ref, *, heads, blk, ndist):
    key = lax.broadcasted_iota(jnp.int32, (blk, blk), 0)
    qry = lax.broadcasted_iota(jnp.int32, (blk, blk), 1)
    for dist in range(ndist):
        @pl.when(pl.program_id(0) == dist)
        def _(dist=dist):
            d = dist * blk + qry - key
            n = jnp.maximum(d, 0)
            nf = jnp.maximum(n, 1).astype(F32)
            large = REL_MAX_EXACT + (jnp.log(nf / REL_MAX_EXACT) / math.log(REL_MAX_DIST / REL_MAX_EXACT)
                                     * (REL_BUCKETS - REL_MAX_EXACT)).astype(jnp.int32)
            large = jnp.minimum(large, REL_BUCKETS - 1)
            bucket = jnp.where(n < REL_MAX_EXACT, n, large)
            b_lo, b_hi = _bucket_range(dist * blk - (blk - 1), dist * blk + (blk - 1))
            masks = [bucket == b for b in range(b_lo, b_hi)]
            for h in range(heads):
                acc = jnp.full((blk, blk), tab_ref[h * REL_BUCKETS + b_hi] * LOG2E, F32)
                for b, mask in zip(range(b_lo, b_hi), masks):
                    acc = jnp.where(mask, tab_ref[h * REL_BUCKETS + b] * LOG2E, acc)
                o_ref[h, 0] = jnp.where(d >= 0, acc, NEG) if dist == 0 else acc


def rel_bias_tiles(rel_table, ndist, blk):
    heads = rel_table.shape[0]
    return pl.pallas_call(
        functools.partial(_bias_tiles_kernel, heads=heads, blk=blk, ndist=ndist),
        out_shape=jax.ShapeDtypeStruct((heads, ndist, blk, blk), F32),
        grid=(ndist,),
        in_specs=[pl.BlockSpec(memory_space=pltpu.SMEM)],
        out_specs=pl.BlockSpec((heads, 1, blk, blk), lambda d: (0, d, 0, 0)),
        compiler_params=_params("parallel"),
        name="rel_bias_tiles",
    )(rel_table.reshape(-1))


MOBA_VT_ROWS = HEAD_DIM + 8


def _moba_kernel(q_ref, k_ref, v_ref, bias_ref, o_ref, kaug_ref, vt_ref, kbar_ref, s_ref, *,
                 nb, blk, topk, scale):
    qi = pl.program_id(2)
    nstream, tq, dh = q_ref.shape
    seq = k_ref.shape[1]
    nbp = kbar_ref.shape[1]

    @pl.when(qi == 0)
    def _():
        row_blk = lax.broadcasted_iota(jnp.int32, (seq, dh), 0) // blk
        col = lax.broadcasted_iota(jnp.int32, (seq, dh), 1)
        onehot = (row_blk == col).astype(BF16)
        ones_rows = (lax.broadcasted_iota(jnp.int32, (MOBA_VT_ROWS - dh, tq), 0) == 0).astype(BF16)
        for r in range(nstream):
            kaug_ref[r, :, 0:dh] = (k_ref[r].astype(F32) * (scale * LOG2E)).astype(BF16)
            kaug_ref[r, :, dh:] = onehot
            for t in range(seq // tq):
                vt_ref[r, t, 0:dh, :] = v_ref[r, t * tq:(t + 1) * tq, :].astype(F32).T.astype(BF16)
                vt_ref[r, t, dh:, :] = ones_rows
            kbar_ref[r] = jnp.zeros(kbar_ref.shape[1:], F32)
            for n in range(nb):
                kblk = k_ref[r, n * blk:(n + 1) * blk, :].astype(F32)
                kbar_ref[r, n:n + 1, :] = jnp.mean(kblk, axis=0, keepdims=True)

    blk_id = lax.broadcasted_iota(jnp.int32, (nbp, tq), 0)
    own = qi * (tq // blk) + lax.broadcasted_iota(jnp.int32, (nbp, tq), 1) // blk
    past = blk_id < own

    def augmented_query(r):
        q = q_ref[r]
        kbar = kbar_ref[r]
        kbar_hi = kbar.astype(BF16)
        kbar_lo = (kbar - kbar_hi.astype(F32)).astype(BF16)
        gate = (lax.dot_general(kbar_hi, q, _NT, preferred_element_type=F32)
                + lax.dot_general(kbar_lo, q, _NT, preferred_element_type=F32))
        gate = jnp.where(past, gate, -jnp.inf)
        rank = jnp.zeros((nbp, tq), jnp.int32)
        for m in range(nb - 1):
            gm = gate[m:m + 1, :]
            before = (gm > gate) | ((gm == gate) & (m < blk_id))
            rank = rank + before.astype(jnp.int32)
        chosen = (past & (rank < topk)) | (blk_id == own)
        pen_t = jnp.where(chosen, 0.0, NEG)
        pen_t = jnp.concatenate([pen_t, jnp.zeros((dh - nbp, tq), F32)], axis=0)
        return jnp.concatenate([q, pen_t.T.astype(BF16)], axis=1)

    q_aug = [augmented_query(r) for r in range(nstream)]

    def produce(r, dist):
        start = pl.multiple_of((qi - dist) * tq, tq)
        s = lax.dot_general(kaug_ref[r, pl.ds(start, tq), :], q_aug[r], _NT,
                            preferred_element_type=F32) + bias_ref[dist]
        s_ref[dist % 2, r] = s
        return jnp.max(s, axis=0, keepdims=True)

    def absorb(r, dist, s_max, m_i, acc):
        m_new = jnp.maximum(m_i, s_max)
        p = jnp.exp2(s_ref[dist % 2, r] - m_new)
        acc = (jnp.exp2(m_i - m_new) * acc
               + jnp.dot(vt_ref[r, qi - dist], p.astype(BF16), preferred_element_type=F32))
        return m_new, acc

    def body(dist, carry):
        done = [absorb(r, dist, *carry[r]) for r in range(nstream)]
        return tuple((produce(r, dist + 1),) + done[r] for r in range(nstream))

    init = tuple((produce(r, 0), jnp.full((1, tq), NEG, F32), jnp.zeros((MOBA_VT_ROWS, tq), F32))
                 for r in range(nstream))
    last = lax.fori_loop(0, qi, body, init)
    for r in range(nstream):
        _, acc = absorb(r, qi, *last[r])
        o_ref[r] = (acc[0:dh] / acc[dh:dh + 1]).T.astype(o_ref.dtype)


def moba_attention(proj3, q_off, k_off, v_off, bias_tiles, heads, *, nstream=2):
    B, S, _ = proj3.shape
    blk, dh = MOBA_BLOCK, HEAD_DIM
    tq = bias_tiles.shape[-1]
    assert S % tq == 0 and tq % blk == 0 and q_off % dh == 0 and k_off % dh == 0 and v_off % dh == 0
    nstream = nstream if B % nstream == 0 else 1
    nb = S // blk
    nbp = -(-nb // 8) * 8
    assert nbp <= dh, "block one-hot columns must fit beside the keys"
    topk = min(MOBA_TOPK, nb)
    qb, kb_, vb_ = q_off // dh, k_off // dh, v_off // dh
    kern = functools.partial(_moba_kernel, nb=nb, blk=blk, topk=topk, scale=dh ** -0.5)
    return pl.pallas_call(
        kern,
        out_shape=jax.ShapeDtypeStruct((B, S, heads * dh), BF16),
        grid=(heads, B // nstream, S // tq),
        in_specs=[pl.BlockSpec((nstream, tq, dh), lambda h, b, i: (b, i, qb + h)),
                  pl.BlockSpec((nstream, S, dh), lambda h, b, i: (b, 0, kb_ + h)),
                  pl.BlockSpec((nstream, S, dh), lambda h, b, i: (b, 0, vb_ + h)),
                  pl.BlockSpec((None, S // tq, tq, tq), lambda h, b, i: (h, 0, 0, 0))],
        out_specs=pl.BlockSpec((nstream, tq, dh), lambda h, b, i: (b, i, h)),
        scratch_shapes=[pltpu.VMEM((nstream, S, 2 * dh), BF16),
                        pltpu.VMEM((nstream, S // tq, MOBA_VT_ROWS, tq), BF16),
                        pltpu.VMEM((nstream, nbp, dh), F32),
                        pltpu.VMEM((2, nstream, tq, tq), F32)],
        compiler_params=_params("parallel", "parallel", "arbitrary"),
        name="moba_attention",
    )(proj3, proj3, proj3, bias_tiles)


def _suffix_sum_matrix(bk):
    r = lax.broadcasted_iota(jnp.int32, (2 * bk, 2 * bk), 0) % bk
    c = lax.broadcasted_iota(jnp.int32, (2 * bk, 2 * bk), 1)
    return ((c >= bk) | (r > c)).astype(BF16)


def _sb_kernel(q_ref, k_ref, v_ref, o_ref, *, scale):
    qi = pl.program_id(2)
    nstream, tq, dh = q_ref.shape
    half = tq // 2
    sums = _suffix_sum_matrix(half)
    q2 = [(q_ref[r].astype(F32) * (scale * LOG2E)).astype(BF16) for r in range(nstream)]

    def tile(r, j, run, acc, valid):
        start = pl.multiple_of(j * tq, tq)
        z = lax.dot_general(q2[r], k_ref[r, pl.ds(start, tq), :], _NT, preferred_element_type=F32)
        sp = jnp.maximum(z, 0.0) + jnp.log2(1.0 + jnp.exp2(-jnp.abs(z)))
        log_1m = -sp
        if valid is not None:
            log_1m = jnp.where(valid, log_1m, 0.0)
        hi = log_1m.astype(BF16)
        lo = (log_1m - hi.astype(F32)).astype(BF16)
        cs_b = jnp.dot(jnp.concatenate([hi[:, half:], lo[:, half:]], axis=1), sums,
                       preferred_element_type=F32)
        cs_a = jnp.dot(jnp.concatenate([hi[:, :half], lo[:, :half]], axis=1), sums,
                       preferred_element_type=F32)
        run_b = run + cs_b[:, half:]
        after = jnp.concatenate([run_b + cs_a[:, :half], run + cs_b[:, :half]], axis=1)
        a = jnp.exp2((z - sp) + after)
        if valid is not None:
            a = jnp.where(valid, a, 0.0)
        acc = acc + jnp.dot(a.astype(BF16), v_ref[r, pl.ds(start, tq), :], preferred_element_type=F32)
        return run_b + cs_a[:, half:], acc

    row = lax.broadcasted_iota(jnp.int32, (tq, tq), 0)
    col = lax.broadcasted_iota(jnp.int32, (tq, tq), 1)
    state = []
    for r in range(nstream):
        run, acc = tile(r, qi, jnp.zeros((tq, half), F32), jnp.zeros((tq, dh), F32), col < row)
        state.append(tile(r, jnp.maximum(qi - 1, 0), run, acc, qi > 0))

    def run_max(state):
        return functools.reduce(jnp.maximum, [jnp.max(run) for run, _ in state])

    def cond(carry):
        j, _, top = carry
        return (j >= 0) & (top > EXP2_ZERO_BELOW)

    def body(carry):
        j, state, _ = carry
        state = tuple(tile(r, j, run, acc, None) for r, (run, acc) in enumerate(state))
        return j - 1, state, run_max(state)

    _, state, _ = lax.while_loop(cond, body, (qi - 2, tuple(state), run_max(state)))
    for r, (_, acc) in enumerate(state):
        o_ref[r] = acc.astype(o_ref.dtype)


def stick_breaking_attention(proj3, q_off, k_off, v_off, heads, *, tq=2 * V7X_LANES, nstream=2):
    B, S, _ = proj3.shape
    dh = HEAD_DIM
    assert tq == 2 * V7X_LANES, "suffix sums work on 128-key halves of a tile"
    assert S % tq == 0 and q_off % dh == 0 and k_off % dh == 0 and v_off % dh == 0
    nstream = nstream if B % nstream == 0 else 1
    qb, kb_, vb_ = q_off // dh, k_off // dh, v_off // dh
    return pl.pallas_call(
        functools.partial(_sb_kernel, scale=dh ** -0.5),
        out_shape=jax.ShapeDtypeStruct((B, S, heads * dh), BF16),
        grid=(heads, B // nstream, S // tq),
        in_specs=[pl.BlockSpec((nstream, tq, dh), lambda h, b, i: (b, i, qb + h)),
                  pl.BlockSpec((nstream, S, dh), lambda h, b, i: (b, 0, kb_ + h)),
                  pl.BlockSpec((nstream, S, dh), lambda h, b, i: (b, 0, vb_ + h))],
        out_specs=pl.BlockSpec((nstream, tq, dh), lambda h, b, i: (b, i, h)),
        compiler_params=_params("parallel", "parallel", "parallel"),
        name="stick_breaking_attention",
    )(proj3, proj3, proj3)


def kernel(x, norm_mix, norm_ffn, w_in, w_pool, pool_scale, w_br_a, w_br_b, w_br_c, w_out, w_gate,
           w_up, w_down, rel_bias, norm_final):
    B, S, D = x.shape
    depth = w_in.shape[0]
    T = B * S
    moba_w = MOBA_HEADS * HEAD_DIM
    sb_w = SB_HEADS * HEAD_DIM
    pool_w = len(POOL_WINDOWS) * POOL_GROUP
    off_qa, off_ka, off_va = 0, moba_w, 2 * moba_w
    off_u = 3 * moba_w
    off_qs = off_u + pool_w
    off_ks, off_vs = off_qs + sb_w, off_qs + 2 * sb_w
    off_gates = off_qs + 3 * sb_w

    bias_tiles = rel_bias_tiles(rel_bias, S // MOBA_QTILE, MOBA_QTILE)
    w_in, w_pool, w_br_a, w_br_b, w_br_c, w_out, w_gate, w_up, w_down = (
        w.astype(BF16) for w in (w_in, w_pool, w_br_a, w_br_b, w_br_c, w_out, w_gate, w_up, w_down))
    h = x.reshape(T, D)
    for l in range(depth):
        proj = norm_matmul(h, norm_mix[l], w_in, l)
        proj3 = proj.reshape(B, S, proj.shape[1])
        ya = moba_attention(proj3, off_qa, off_ka, off_va, bias_tiles, MOBA_HEADS).reshape(T, moba_w)
        yb = pool_mixer(proj, off_u, w_pool, l, pool_scale[l], S)
        yc = stick_breaking_attention(proj3, off_qs, off_ks, off_vs, SB_HEADS).reshape(T, sb_w)
        m = merge_branches(ya, yb, yc, proj, off_gates, w_br_a, w_br_b, w_br_c, l)
        h = matmul_res(m, w_out, l, h)
        act = ffn_up(h, norm_ffn[l], w_gate, w_up, l)
        h = matmul_res(act, w_down, l, h)
    return final_norm(h, norm_final).reshape(B, S, D)
```

```python
import functools
import math

import jax
import jax.numpy as jnp
from jax import lax
from jax.experimental import pallas as pl
from jax.experimental.pallas import tpu as pltpu

F32 = jnp.float32
BF16 = jnp.bfloat16

HEAD_DIM = 128
MOBA_HEADS = 8
MOBA_BLOCK = 256
MOBA_TOPK = 3
MOBA_QTILE = 512
SB_HEADS = 8
POOL_WINDOWS = (2, 4, 8, 16)
POOL_GROUP = 256
POOL_HALO = 16
REL_BUCKETS = 32
REL_MAX_EXACT = 16
REL_MAX_DIST = 2048
EPS = 1e-6
NEG = -1e30
LOG2E = math.log2(math.e)
EXP2_ZERO_BELOW = -152.0

V7X_LANES = 128
V7X_VMEM_LIMIT_BYTES = 56 * 1024 * 1024

_NT = (((1,), (1,)), ((), ()))


def _params(*semantics):
    return pltpu.CompilerParams(dimension_semantics=semantics,
                                vmem_limit_bytes=V7X_VMEM_LIMIT_BYTES)


def _pick_tile(n, want, align):
    t = min(want, n)
    while t > align and (n % t or t % align):
        t -= align
    assert n % t == 0 and t % align == 0, (n, want, align)
    return t


def _rms_normalize(x, g):
    ms = jnp.mean(x * x, axis=-1, keepdims=True)
    return x * lax.rsqrt(ms + EPS) * g


def _norm_matmul_kernel(x_ref, g_ref, w_ref, o_ref, xn_ref):
    @pl.when(pl.program_id(1) == 0)
    def _():
        xn_ref[...] = _rms_normalize(x_ref[...], g_ref[...]).astype(BF16)

    o_ref[...] = jnp.dot(xn_ref[...], w_ref[...].astype(BF16),
                         preferred_element_type=F32).astype(o_ref.dtype)


def norm_matmul(x, g, w, layer, *, out_dtype=BF16, tm=1024, tn=1024):
    T, K = x.shape
    N = w.shape[2]
    tm = _pick_tile(T, tm, 16)
    tn = _pick_tile(N, tn, V7X_LANES)
    return pl.pallas_call(
        _norm_matmul_kernel,
        out_shape=jax.ShapeDtypeStruct((T, N), out_dtype),
        grid=(T // tm, N // tn),
        in_specs=[pl.BlockSpec((tm, K), lambda i, j: (i, 0)),
                  pl.BlockSpec((1, K), lambda i, j: (0, 0)),
                  pl.BlockSpec((None, K, tn), lambda i, j: (layer, 0, j))],
        out_specs=pl.BlockSpec((tm, tn), lambda i, j: (i, j)),
        scratch_shapes=[pltpu.VMEM((tm, K), BF16)],
        compiler_params=_params("parallel", "arbitrary"),
        name="norm_matmul",
    )(x, g.reshape(1, K), w)


def _ffn_up_kernel(x_ref, g_ref, wg_ref, wu_ref, o_ref, xn_ref):
    @pl.when(pl.program_id(1) == 0)
    def _():
        xn_ref[...] = _rms_normalize(x_ref[...], g_ref[...]).astype(BF16)

    xn = xn_ref[...]
    a = jnp.dot(xn, wg_ref[...].astype(BF16), preferred_element_type=F32)
    b = jnp.dot(xn, wu_ref[...].astype(BF16), preferred_element_type=F32)
    o_ref[...] = (a * jax.nn.sigmoid(a) * b).astype(o_ref.dtype)


def ffn_up(x, g, wg, wu, layer, *, tm=1024, tn=512):
    T, K = x.shape
    N = wg.shape[2]
    tm = _pick_tile(T, tm, 16)
    tn = _pick_tile(N, tn, V7X_LANES)
    return pl.pallas_call(
        _ffn_up_kernel,
        out_shape=jax.ShapeDtypeStruct((T, N), BF16),
        grid=(T // tm, N // tn),
        in_specs=[pl.BlockSpec((tm, K), lambda i, j: (i, 0)),
                  pl.BlockSpec((1, K), lambda i, j: (0, 0)),
                  pl.BlockSpec((None, K, tn), lambda i, j: (layer, 0, j)),
                  pl.BlockSpec((None, K, tn), lambda i, j: (layer, 0, j))],
        out_specs=pl.BlockSpec((tm, tn), lambda i, j: (i, j)),
        scratch_shapes=[pltpu.VMEM((tm, K), BF16)],
        compiler_params=_params("parallel", "arbitrary"),
        name="ffn_up",
    )(x, g.reshape(1, K), wg, wu)


def _matmul_res_kernel(x_ref, w_ref, r_ref, o_ref):
    o_ref[...] = r_ref[...] + jnp.dot(x_ref[...], w_ref[...], preferred_element_type=F32)


def matmul_res(x, w, layer, res, *, tm=1024, tn=512):
    T, K = x.shape
    N = w.shape[2]
    tm = _pick_tile(T, tm, 16)
    tn = _pick_tile(N, tn, V7X_LANES)
    return pl.pallas_call(
        _matmul_res_kernel,
        out_shape=jax.ShapeDtypeStruct((T, N), F32),
        grid=(T // tm, N // tn),
        in_specs=[pl.BlockSpec((tm, K), lambda i, j: (i, 0)),
                  pl.BlockSpec((None, K, tn), lambda i, j: (layer, 0, j)),
                  pl.BlockSpec((tm, tn), lambda i, j: (i, j))],
        out_specs=pl.BlockSpec((tm, tn), lambda i, j: (i, j)),
        compiler_params=_params("parallel", "parallel"),
        name="matmul_res",
    )(x, w, res)


def _merge_kernel(ya_ref, yb_ref, yc_ref, ga_ref, gb_ref, gc_ref, wa_ref, wb_ref, wc_ref, o_ref):
    def branch(y_ref, g_ref, w_ref):
        gate = jax.nn.sigmoid(g_ref[...].astype(F32))
        return gate * jnp.dot(y_ref[...], w_ref[...], preferred_element_type=F32)

    m = branch(ya_ref, ga_ref, wa_ref) + branch(yb_ref, gb_ref, wb_ref) + branch(yc_ref, gc_ref, wc_ref)
    o_ref[...] = m.astype(o_ref.dtype)


def merge_branches(ya, yb, yc, proj, gate_off, wa, wb, wc, layer, *, tm=1024, tn=512):
    T, K = ya.shape
    N = wa.shape[2]
    tm = _pick_tile(T, tm, 16)
    tn = _pick_tile(N, tn, V7X_LANES)
    assert gate_off % tn == 0
    y_spec = pl.BlockSpec((tm, K), lambda i, j: (i, 0))
    w_spec = pl.BlockSpec((None, K, tn), lambda i, j: (layer, 0, j))

    def gate_spec(branch):
        base = (gate_off + branch * N) // tn
        return pl.BlockSpec((tm, tn), lambda i, j: (i, base + j))

    return pl.pallas_call(
        _merge_kernel,
        out_shape=jax.ShapeDtypeStruct((T, N), BF16),
        grid=(T // tm, N // tn),
        in_specs=[y_spec, y_spec, y_spec, gate_spec(0), gate_spec(1), gate_spec(2),
                  w_spec, w_spec, w_spec],
        out_specs=pl.BlockSpec((tm, tn), lambda i, j: (i, j)),
        compiler_params=_params("parallel", "parallel"),
        name="merge_branches",
    )(ya, yb, yc, proj, proj, proj, wa, wb, wc)


def _final_norm_kernel(x_ref, g_ref, o_ref):
    o_ref[...] = _rms_normalize(x_ref[...], g_ref[...])


def final_norm(x, g, *, tm=512):
    T, K = x.shape
    tm = _pick_tile(T, tm, 8)
    return pl.pallas_call(
        _final_norm_kernel,
        out_shape=jax.ShapeDtypeStruct((T, K), F32),
        grid=(T // tm,),
        in_specs=[pl.BlockSpec((tm, K), lambda i: (i, 0)),
                  pl.BlockSpec((1, K), lambda i: (0, 0))],
        out_specs=pl.BlockSpec((tm, K), lambda i: (i, 0)),
        compiler_params=_params("parallel"),
        name="final_norm",
    )(x, g.reshape(1, K))


def _pool_kernel(u_ref, halo_ref, w_ref, sc_ref, o_ref, ext_ref, *, tm, tiles_per_seq):
    t_in_seq = pl.program_id(0) % tiles_per_seq
    halo = halo_ref[...].astype(F32)
    ext_ref[0:POOL_HALO, :] = jnp.where(t_in_seq == 0, 0.0, halo)
    ext_ref[POOL_HALO:, :] = u_ref[...].astype(F32)
    pos = t_in_seq * tm + lax.broadcasted_iota(jnp.int32, (tm, 1), 0)
    for gi, win in enumerate(POOL_WINDOWS):
        cols = slice(gi * POOL_GROUP, (gi + 1) * POOL_GROUP)
        x = ext_ref[POOL_HALO:, cols]
        total = x
        for back in range(1, win):
            total = total + ext_ref[pl.ds(POOL_HALO - back, tm), cols]
        cnt = jnp.minimum(pos + 1, win).astype(F32)
        pooled = total / cnt - x
        y = jnp.dot(pooled.astype(BF16), w_ref[gi], preferred_element_type=F32)
        o_ref[:, cols] = (y * sc_ref[:, cols]).astype(o_ref.dtype)


def pool_mixer(proj, u_off, w_grp, layer, scale, seq_len, *, tm=512):
    T = proj.shape[0]
    width = len(POOL_WINDOWS) * POOL_GROUP
    tm = _pick_tile(seq_len, tm, POOL_HALO)
    assert u_off % width == 0 and max(POOL_WINDOWS) <= POOL_HALO
    ublk = u_off // width
    rows_per_tile = tm // POOL_HALO
    return pl.pallas_call(
        functools.partial(_pool_kernel, tm=tm, tiles_per_seq=seq_len // tm),
        out_shape=jax.ShapeDtypeStruct((T, width), BF16),
        grid=(T // tm,),
        in_specs=[pl.BlockSpec((tm, width), lambda i: (i, ublk)),
                  pl.BlockSpec((POOL_HALO, width),
                               lambda i: (jnp.maximum(i * rows_per_tile - 1, 0), ublk)),
                  pl.BlockSpec((None,) + w_grp.shape[1:], lambda i: (layer, 0, 0, 0)),
                  pl.BlockSpec((1, width), lambda i: (0, 0))],
        out_specs=pl.BlockSpec((tm, width), lambda i: (i, 0)),
        scratch_shapes=[pltpu.VMEM((tm + POOL_HALO, width), F32)],
        compiler_params=_params("parallel"),
        name="pool_mixer",
    )(proj, proj, w_grp, scale.reshape(1, width))


def _bucket_range(d_lo, d_hi):
    def bucket(d):
        if d < REL_MAX_EXACT:
            return d
        big = REL_MAX_EXACT + int(math.log(d / REL_MAX_EXACT) / math.log(REL_MAX_DIST / REL_MAX_EXACT)
                                  * (REL_BUCKETS - REL_MAX_EXACT))
        return min(big, REL_BUCKETS - 1)
    return max(bucket(max(d_lo, 0)) - 1, 0), min(bucket(d_hi) + 1, REL_BUCKETS - 1)


def _bias_tiles_kernel(tab_ref, o_ref, *, heads, blk, ndist):
    key = lax.broadcasted_iota(jnp.int32, (blk, blk), 0)
    qry = lax.broadcasted_iota(jnp.int32, (blk, blk), 1)
    for dist in range(ndist):
        @pl.when(pl.program_id(0) == dist)
        def _(dist=dist):
            d = dist * blk + qry - key
            n = jnp.maximum(d, 0)
            nf = jnp.maximum(n, 1).astype(F32)
            large = REL_MAX_EXACT + (jnp.log(nf / REL_MAX_EXACT) / math.log(REL_MAX_DIST / REL_MAX_EXACT)
                                     * (REL_BUCKETS - REL_MAX_EXACT)).astype(jnp.int32)
            large = jnp.minimum(large, REL_BUCKETS - 1)
            bucket = jnp.where(n < REL_MAX_EXACT, n, large)
            b_lo, b_hi = _bucket_range(dist * blk - (blk - 1), dist * blk + (blk - 1))
            masks = [bucket == b for b in range(b_lo, b_hi)]
            for h in range(heads):
                acc = jnp.full((blk, blk), tab_ref[h * REL_BUCKETS + b_hi] * LOG2E, F32)
                for b, mask in zip(range(b_lo, b_hi), masks):
                    acc = jnp.where(mask, tab_ref[h * REL_BUCKETS + b] * LOG2E, acc)
                o_ref[h, 0] = jnp.where(d >= 0, acc, NEG) if dist == 0 else acc


def rel_bias_tiles(rel_table, ndist, blk):
    heads = rel_table.shape[0]
    return pl.pallas_call(
        functools.partial(_bias_tiles_kernel, heads=heads, blk=blk, ndist=ndist),
        out_shape=jax.ShapeDtypeStruct((heads, ndist, blk, blk), F32),
        grid=(ndist,),
        in_specs=[pl.BlockSpec(memory_space=pltpu.SMEM)],
        out_specs=pl.BlockSpec((heads, 1, blk, blk), lambda d: (0, d, 0, 0)),
        compiler_params=_params("parallel"),
        name="rel_bias_tiles",
    )(rel_table.reshape(-1))


MOBA_VT_ROWS = HEAD_DIM + 8


def _moba_kernel(q_ref, k_ref, v_ref, bias_ref, o_ref, kaug_ref, vt_ref, kbar_ref, s_ref, *,
                 nb, blk, topk, scale):
    qi = pl.program_id(2)
    nstream, tq, dh = q_ref.shape
    seq = k_ref.shape[1]
    nbp = kbar_ref.shape[1]

    @pl.when(qi == 0)
    def _():
        row_blk = lax.broadcasted_iota(jnp.int32, (seq, dh), 0) // blk
        col = lax.broadcasted_iota(jnp.int32, (seq, dh), 1)
        onehot = (row_blk == col).astype(BF16)
        ones_rows = (lax.broadcasted_iota(jnp.int32, (MOBA_VT_ROWS - dh, tq), 0) == 0).astype(BF16)
        for r in range(nstream):
            kaug_ref[r, :, 0:dh] = (k_ref[r].astype(F32) * (scale * LOG2E)).astype(BF16)
            kaug_ref[r, :, dh:] = onehot
            for t in range(seq // tq):
                vt_ref[r, t, 0:dh, :] = v_ref[r, t * tq:(t + 1) * tq, :].astype(F32).T.astype(BF16)
                vt_ref[r, t, dh:, :] = ones_rows
            kbar_ref[r] = jnp.zeros(kbar_ref.shape[1:], F32)
            for n in range(nb):
                kblk = k_ref[r, n * blk:(n + 1) * blk, :].astype(F32)
                kbar_ref[r, n:n + 1, :] = jnp.mean(kblk, axis=0, keepdims=True)

    blk_id = lax.broadcasted_iota(jnp.int32, (nbp, tq), 0)
    own = qi * (tq // blk) + lax.broadcasted_iota(jnp.int32, (nbp, tq), 1) // blk
    past = blk_id < own

    def augmented_query(r):
        q = q_ref[r]
        kbar = kbar_ref[r]
        kbar_hi = kbar.astype(BF16)
        kbar_lo = (kbar - kbar_hi.astype(F32)).astype(BF16)
        gate = (lax.dot_general(kbar_hi, q, _NT, preferred_element_type=F32)
                + lax.dot_general(kbar_lo, q, _NT, preferred_element_type=F32))
        gate = jnp.where(past, gate, -jnp.inf)
        rank = jnp.zeros((nbp, tq), jnp.int32)
        for m in range(nb - 1):
            gm = gate[m:m + 1, :]
            before = (gm > gate) | ((gm == gate) & (m < blk_id))
            rank = rank + before.astype(jnp.int32)
        chosen = (past & (rank < topk)) | (blk_id == own)
        pen_t = jnp.where(chosen, 0.0, NEG)
        pen_t = jnp.concatenate([pen_t, jnp.zeros((dh - nbp, tq), F32)], axis=0)
        return jnp.concatenate([q, pen_t.T.astype(BF16)], axis=1)

    q_aug = [augmented_query(r) for r in range(nstream)]

    def produce(r, dist):
        start = pl.multiple_of((qi - dist) * tq, tq)
        s = lax.dot_general(kaug_ref[r, pl.ds(start, tq), :], q_aug[r], _NT,
                            preferred_element_type=F32) + bias_ref[dist]
        s_ref[dist % 2, r] = s
        return jnp.max(s, axis=0, keepdims=True)

    def absorb(r, dist, s_max, m_i, acc):
        m_new = jnp.maximum(m_i, s_max)
        p = jnp.exp2(s_ref[dist % 2, r] - m_new)
        acc = (jnp.exp2(m_i - m_new) * acc
               + jnp.dot(vt_ref[r, qi - dist], p.astype(BF16), preferred_element_type=F32))
        return m_new, acc

    def body(dist, carry):
        done = [absorb(r, dist, *carry[r]) for r in range(nstream)]
        return tuple((produce(r, dist + 1),) + done[r] for r in range(nstream))

    init = tuple((produce(r, 0), jnp.full((1, tq), NEG, F32), jnp.zeros((MOBA_VT_ROWS, tq), F32))
                 for r in range(nstream))
    last = lax.fori_loop(0, qi, body, init)
    for r in range(nstream):
        _, acc = absorb(r, qi, *last[r])
        o_ref[r] = (acc[0:dh] / acc[dh:dh + 1]).T.astype(o_ref.dtype)


def moba_attention(proj3, q_off, k_off, v_off, bias_tiles, heads, *, nstream=4):
    B, S, _ = proj3.shape
    blk, dh = MOBA_BLOCK, HEAD_DIM
    tq = bias_tiles.shape[-1]
    assert S % tq == 0 and tq % blk == 0 and q_off % dh == 0 and k_off % dh == 0 and v_off % dh == 0
    nstream = nstream if B % nstream == 0 else 1
    nb = S // blk
    nbp = -(-nb // 8) * 8
    assert nbp <= dh, "block one-hot columns must fit beside the keys"
    topk = min(MOBA_TOPK, nb)
    qb, kb_, vb_ = q_off // dh, k_off // dh, v_off // dh
    kern = functools.partial(_moba_kernel, nb=nb, blk=blk, topk=topk, scale=dh ** -0.5)
    return pl.pallas_call(
        kern,
        out_shape=jax.ShapeDtypeStruct((B, S, heads * dh), BF16),
        grid=(heads, B // nstream, S // tq),
        in_specs=[pl.BlockSpec((nstream, tq, dh), lambda h, b, i: (b, i, qb + h)),
                  pl.BlockSpec((nstream, S, dh), lambda h, b, i: (b, 0, kb_ + h),
                               pipeline_mode=pl.Buffered(1)),
                  pl.BlockSpec((nstream, S, dh), lambda h, b, i: (b, 0, vb_ + h),
                               pipeline_mode=pl.Buffered(1)),
                  pl.BlockSpec((None, S // tq, tq, tq), lambda h, b, i: (h, 0, 0, 0),
                               pipeline_mode=pl.Buffered(1))],
        out_specs=pl.BlockSpec((nstream, tq, dh), lambda h, b, i: (b, i, h)),
        scratch_shapes=[pltpu.VMEM((nstream, S, 2 * dh), BF16),
                        pltpu.VMEM((nstream, S // tq, MOBA_VT_ROWS, tq), BF16),
                        pltpu.VMEM((nstream, nbp, dh), F32),
                        pltpu.VMEM((2, nstream, tq, tq), F32)],
        compiler_params=_params("parallel", "parallel", "arbitrary"),
        name="moba_attention",
    )(proj3, proj3, proj3, bias_tiles)


def _suffix_sum_matrix(bk):
    r = lax.broadcasted_iota(jnp.int32, (2 * bk, 2 * bk), 0) % bk
    c = lax.broadcasted_iota(jnp.int32, (2 * bk, 2 * bk), 1)
    return ((c >= bk) | (r > c)).astype(BF16)


def _sb_kernel(q_ref, k_ref, v_ref, o_ref, *, scale):
    qi = pl.program_id(2)
    nstream, tq, dh = q_ref.shape
    half = tq // 2
    sums = _suffix_sum_matrix(half)
    q2 = [(q_ref[r].astype(F32) * (scale * LOG2E)).astype(BF16) for r in range(nstream)]

    def tile(r, j, run, acc, valid):
        start = pl.multiple_of(j * tq, tq)
        z = lax.dot_general(q2[r], k_ref[r, pl.ds(start, tq), :], _NT, preferred_element_type=F32)
        sp = jnp.maximum(z, 0.0) + jnp.log2(1.0 + jnp.exp2(-jnp.abs(z)))
        log_1m = -sp
        if valid is not None:
            log_1m = jnp.where(valid, log_1m, 0.0)
        hi = log_1m.astype(BF16)
        lo = (log_1m - hi.astype(F32)).astype(BF16)
        cs_b = jnp.dot(jnp.concatenate([hi[:, half:], lo[:, half:]], axis=1), sums,
                       preferred_element_type=F32)
        cs_a = jnp.dot(jnp.concatenate([hi[:, :half], lo[:, :half]], axis=1), sums,
                       preferred_element_type=F32)
        run_b = run + cs_b[:, half:]
        after = jnp.concatenate([run_b + cs_a[:, :half], run + cs_b[:, :half]], axis=1)
        a = jnp.exp2((z - sp) + after)
        if valid is not None:
            a = jnp.where(valid, a, 0.0)
        acc = acc + jnp.dot(a.astype(BF16), v_ref[r, pl.ds(start, tq), :], preferred_element_type=F32)
        return run_b + cs_a[:, half:], acc

    row = lax.broadcasted_iota(jnp.int32, (tq, tq), 0)
    col = lax.broadcasted_iota(jnp.int32, (tq, tq), 1)
    no_prev = jnp.where(qi > 0, 0.0, 4.0 * EXP2_ZERO_BELOW)
    state = []
    for r in range(nstream):
        run, acc = tile(r, qi, jnp.zeros((tq, half), F32), jnp.zeros((tq, dh), F32), col < row)
        state.append(tile(r, jnp.maximum(qi - 1, 0), run + no_prev, acc, None))

    def run_max(state):
        return functools.reduce(jnp.maximum, [jnp.max(run) for run, _ in state])

    def cond(carry):
        j, _, top = carry
        return (j >= 0) & (top > EXP2_ZERO_BELOW)

    def body(carry):
        j, state, _ = carry
        state = tuple(tile(r, j, run, acc, None) for r, (run, acc) in enumerate(state))
        return j - 1, state, run_max(state)

    _, state, _ = lax.while_loop(cond, body, (qi - 2, tuple(state), run_max(state)))
    for r, (_, acc) in enumerate(state):
        o_ref[r] = acc.astype(o_ref.dtype)


def stick_breaking_attention(proj3, q_off, k_off, v_off, heads, *, tq=2 * V7X_LANES, nstream=4):
    B, S, _ = proj3.shape
    dh = HEAD_DIM
    assert tq == 2 * V7X_LANES, "suffix sums work on 128-key halves of a tile"
    assert S % tq == 0 and q_off % dh == 0 and k_off % dh == 0 and v_off % dh == 0
    nstream = nstream if B % nstream == 0 else 1
    qb, kb_, vb_ = q_off // dh, k_off // dh, v_off // dh
    return pl.pallas_call(
        functools.partial(_sb_kernel, scale=dh ** -0.5),
        out_shape=jax.ShapeDtypeStruct((B, S, heads * dh), BF16),
        grid=(heads, B // nstream, S // tq),
        in_specs=[pl.BlockSpec((nstream, tq, dh), lambda h, b, i: (b, i, qb + h)),
                  pl.BlockSpec((nstream, S, dh), lambda h, b, i: (b, 0, kb_ + h)),
                  pl.BlockSpec((nstream, S, dh), lambda h, b, i: (b, 0, vb_ + h))],
        out_specs=pl.BlockSpec((nstream, tq, dh), lambda h, b, i: (b, i, h)),
        compiler_params=_params("parallel", "parallel", "parallel"),
        name="stick_breaking_attention",
    )(proj3, proj3, proj3)


def kernel(x, norm_mix, norm_ffn, w_in, w_pool, pool_scale, w_br_a, w_br_b, w_br_c, w_out, w_gate,
           w_up, w_down, rel_bias, norm_final):
    B, S, D = x.shape
    depth = w_in.shape[0]
    T = B * S
    moba_w = MOBA_HEADS * HEAD_DIM
    sb_w = SB_HEADS * HEAD_DIM
    pool_w = len(POOL_WINDOWS) * POOL_GROUP
    off_qa, off_ka, off_va = 0, moba_w, 2 * moba_w
    off_u = 3 * moba_w
    off_qs = off_u + pool_w
    off_ks, off_vs = off_qs + sb_w, off_qs + 2 * sb_w
    off_gates = off_qs + 3 * sb_w

    bias_tiles = rel_bias_tiles(rel_bias, S // MOBA_QTILE, MOBA_QTILE)
    w_pool, w_br_a, w_br_b, w_br_c, w_out, w_down = (
        w.astype(BF16) for w in (w_pool, w_br_a, w_br_b, w_br_c, w_out, w_down))
    h = x.reshape(T, D)
    for l in range(depth):
        proj = norm_matmul(h, norm_mix[l], w_in, l)
        proj3 = proj.reshape(B, S, proj.shape[1])
        ya = moba_attention(proj3, off_qa, off_ka, off_va, bias_tiles, MOBA_HEADS).reshape(T, moba_w)
        yb = pool_mixer(proj, off_u, w_pool, l, pool_scale[l], S)
        yc = stick_breaking_attention(proj3, off_qs, off_ks, off_vs, SB_HEADS).reshape(T, sb_w)
        m = merge_branches(ya, yb, yc, proj, off_gates, w_br_a, w_br_b, w_br_c, l)
        h = matmul_res(m, w_out, l, h)
        act = ffn_up(h, norm_ffn[l], w_gate, w_up, l)
        h = matmul_res(act, w_down, l, h)
    return final_norm(h, norm_final).reshape(B, S, D)
```

```python
import functools
import math

import jax
import jax.numpy as jnp
from jax import lax
from jax.experimental import pallas as pl
from jax.experimental.pallas import tpu as pltpu

F32 = jnp.float32
BF16 = jnp.bfloat16

HEAD_DIM = 128
MOBA_HEADS = 8
MOBA_BLOCK = 256
MOBA_TOPK = 3
MOBA_QTILE = 512
SB_HEADS = 8
POOL_WINDOWS = (2, 4, 8, 16)
POOL_GROUP = 256
POOL_HALO = 16
REL_BUCKETS = 32
REL_MAX_EXACT = 16
REL_MAX_DIST = 2048
EPS = 1e-6
NEG = -1e30
LOG2E = math.log2(math.e)
EXP2_ZERO_BELOW = -152.0

V7X_LANES = 128
V7X_VMEM_LIMIT_BYTES = 56 * 1024 * 1024

_NT = (((1,), (1,)), ((), ()))


def _params(*semantics):
    return pltpu.CompilerParams(dimension_semantics=semantics,
                                vmem_limit_bytes=V7X_VMEM_LIMIT_BYTES)


def _pick_tile(n, want, align):
    t = min(want, n)
    while t > align and (n % t or t % align):
        t -= align
    assert n % t == 0 and t % align == 0, (n, want, align)
    return t


def _rms_normalize(x, g):
    ms = jnp.mean(x * x, axis=-1, keepdims=True)
    return x * lax.rsqrt(ms + EPS) * g


NORM_CHUNK = 256


def _norm_matmul_kernel(x_ref, g_ref, w_ref, o_ref, xn_ref):
    @pl.when(pl.program_id(1) == 0)
    def _():
        w = w_ref[...].astype(BF16)
        for r0 in range(0, x_ref.shape[0], NORM_CHUNK):
            rows = pl.ds(r0, NORM_CHUNK)
            xn = _rms_normalize(x_ref[rows, :], g_ref[...]).astype(BF16)
            xn_ref[rows, :] = xn
            o_ref[rows, :] = jnp.dot(xn, w, preferred_element_type=F32).astype(o_ref.dtype)

    @pl.when(pl.program_id(1) > 0)
    def _():
        o_ref[...] = jnp.dot(xn_ref[...], w_ref[...].astype(BF16),
                             preferred_element_type=F32).astype(o_ref.dtype)


def norm_matmul(x, g, w, layer, *, out_dtype=BF16, tm=1024, tn=1024):
    T, K = x.shape
    N = w.shape[2]
    tm = _pick_tile(T, tm, 16)
    tn = _pick_tile(N, tn, V7X_LANES)
    return pl.pallas_call(
        _norm_matmul_kernel,
        out_shape=jax.ShapeDtypeStruct((T, N), out_dtype),
        grid=(T // tm, N // tn),
        in_specs=[pl.BlockSpec((tm, K), lambda i, j: (i, 0)),
                  pl.BlockSpec((1, K), lambda i, j: (0, 0)),
                  pl.BlockSpec((None, K, tn), lambda i, j: (layer, 0, j))],
        out_specs=pl.BlockSpec((tm, tn), lambda i, j: (i, j)),
        scratch_shapes=[pltpu.VMEM((tm, K), BF16)],
        compiler_params=_params("parallel", "arbitrary"),
        name="norm_matmul",
    )(x, g.reshape(1, K), w)


def _ffn_up_kernel(x_ref, g_ref, wg_ref, wu_ref, o_ref, xn_ref):
    def swiglu(xn, wg, wu):
        a = jnp.dot(xn, wg, preferred_element_type=F32)
        b = jnp.dot(xn, wu, preferred_element_type=F32)
        return (a * jax.nn.sigmoid(a) * b).astype(o_ref.dtype)

    @pl.when(pl.program_id(1) == 0)
    def _():
        wg, wu = wg_ref[...].astype(BF16), wu_ref[...].astype(BF16)
        for r0 in range(0, x_ref.shape[0], NORM_CHUNK):
            rows = pl.ds(r0, NORM_CHUNK)
            xn = _rms_normalize(x_ref[rows, :], g_ref[...]).astype(BF16)
            xn_ref[rows, :] = xn
            o_ref[rows, :] = swiglu(xn, wg, wu)

    @pl.when(pl.program_id(1) > 0)
    def _():
        o_ref[...] = swiglu(xn_ref[...], wg_ref[...].astype(BF16), wu_ref[...].astype(BF16))


def ffn_up(x, g, wg, wu, layer, *, tm=1024, tn=512):
    T, K = x.shape
    N = wg.shape[2]
    tm = _pick_tile(T, tm, 16)
    tn = _pick_tile(N, tn, V7X_LANES)
    return pl.pallas_call(
        _ffn_up_kernel,
        out_shape=jax.ShapeDtypeStruct((T, N), BF16),
        grid=(T // tm, N // tn),
        in_specs=[pl.BlockSpec((tm, K), lambda i, j: (i, 0)),
                  pl.BlockSpec((1, K), lambda i, j: (0, 0)),
                  pl.BlockSpec((None, K, tn), lambda i, j: (layer, 0, j)),
                  pl.BlockSpec((None, K, tn), lambda i, j: (layer, 0, j))],
        out_specs=pl.BlockSpec((tm, tn), lambda i, j: (i, j)),
        scratch_shapes=[pltpu.VMEM((tm, K), BF16)],
        compiler_params=_params("parallel", "arbitrary"),
        name="ffn_up",
    )(x, g.reshape(1, K), wg, wu)


def _matmul_res_kernel(x_ref, w_ref, r_ref, o_ref):
    o_ref[...] = r_ref[...] + jnp.dot(x_ref[...], w_ref[...], preferred_element_type=F32)


def matmul_res(x, w, layer, res, *, tm=1024, tn=512):
    T, K = x.shape
    N = w.shape[2]
    tm = _pick_tile(T, tm, 16)
    tn = _pick_tile(N, tn, V7X_LANES)
    return pl.pallas_call(
        _matmul_res_kernel,
        out_shape=jax.ShapeDtypeStruct((T, N), F32),
        grid=(T // tm, N // tn),
        in_specs=[pl.BlockSpec((tm, K), lambda i, j: (i, 0)),
                  pl.BlockSpec((None, K, tn), lambda i, j: (layer, 0, j)),
                  pl.BlockSpec((tm, tn), lambda i, j: (i, j))],
        out_specs=pl.BlockSpec((tm, tn), lambda i, j: (i, j)),
        compiler_params=_params("parallel", "parallel"),
        name="matmul_res",
    )(x, w, res)


def _merge_kernel(ya_ref, yb_ref, yc_ref, ga_ref, gb_ref, gc_ref, wa_ref, wb_ref, wc_ref, o_ref):
    def branch(y_ref, g_ref, w_ref):
        gate = jax.nn.sigmoid(g_ref[...].astype(F32))
        return gate * jnp.dot(y_ref[...], w_ref[...], preferred_element_type=F32)

    m = branch(ya_ref, ga_ref, wa_ref) + branch(yb_ref, gb_ref, wb_ref) + branch(yc_ref, gc_ref, wc_ref)
    o_ref[...] = m.astype(o_ref.dtype)


def merge_branches(ya, yb, yc, proj, gate_off, wa, wb, wc, layer, *, tm=1024, tn=1024):
    T, K = ya.shape
    N = wa.shape[2]
    tm = _pick_tile(T, tm, 16)
    tn = _pick_tile(N, tn, V7X_LANES)
    assert gate_off % tn == 0
    y_spec = pl.BlockSpec((tm, K), lambda i, j: (i, 0))
    w_spec = pl.BlockSpec((None, K, tn), lambda i, j: (layer, 0, j))

    def gate_spec(branch):
        base = (gate_off + branch * N) // tn
        return pl.BlockSpec((tm, tn), lambda i, j: (i, base + j))

    return pl.pallas_call(
        _merge_kernel,
        out_shape=jax.ShapeDtypeStruct((T, N), BF16),
        grid=(T // tm, N // tn),
        in_specs=[y_spec, y_spec, y_spec, gate_spec(0), gate_spec(1), gate_spec(2),
                  w_spec, w_spec, w_spec],
        out_specs=pl.BlockSpec((tm, tn), lambda i, j: (i, j)),
        compiler_params=_params("parallel", "parallel"),
        name="merge_branches",
    )(ya, yb, yc, proj, proj, proj, wa, wb, wc)


def _final_norm_kernel(x_ref, g_ref, o_ref):
    o_ref[...] = _rms_normalize(x_ref[...], g_ref[...])


def final_norm(x, g, *, tm=512):
    T, K = x.shape
    tm = _pick_tile(T, tm, 8)
    return pl.pallas_call(
        _final_norm_kernel,
        out_shape=jax.ShapeDtypeStruct((T, K), F32),
        grid=(T // tm,),
        in_specs=[pl.BlockSpec((tm, K), lambda i: (i, 0)),
                  pl.BlockSpec((1, K), lambda i: (0, 0))],
        out_specs=pl.BlockSpec((tm, K), lambda i: (i, 0)),
        compiler_params=_params("parallel"),
        name="final_norm",
    )(x, g.reshape(1, K))


def _pool_kernel(u_ref, halo_ref, w_ref, sc_ref, o_ref, ext_ref, *, tm, tiles_per_seq):
    t_in_seq = pl.program_id(0) % tiles_per_seq
    halo = halo_ref[...].astype(F32)
    ext_ref[0:POOL_HALO, :] = jnp.where(t_in_seq == 0, 0.0, halo)
    ext_ref[POOL_HALO:, :] = u_ref[...].astype(F32)
    pos = t_in_seq * tm + lax.broadcasted_iota(jnp.int32, (tm, 1), 0)
    for gi, win in enumerate(POOL_WINDOWS):
        cols = slice(gi * POOL_GROUP, (gi + 1) * POOL_GROUP)
        x = ext_ref[POOL_HALO:, cols]
        total = x
        for back in range(1, win):
            total = total + ext_ref[pl.ds(POOL_HALO - back, tm), cols]
        cnt = jnp.minimum(pos + 1, win).astype(F32)
        pooled = total / cnt - x
        y = jnp.dot(pooled.astype(BF16), w_ref[gi], preferred_element_type=F32)
        o_ref[:, cols] = (y * sc_ref[:, cols]).astype(o_ref.dtype)


def pool_mixer(proj, u_off, w_grp, layer, scale, seq_len, *, tm=1024):
    T = proj.shape[0]
    width = len(POOL_WINDOWS) * POOL_GROUP
    tm = _pick_tile(seq_len, tm, POOL_HALO)
    assert u_off % width == 0 and max(POOL_WINDOWS) <= POOL_HALO
    ublk = u_off // width
    rows_per_tile = tm // POOL_HALO
    return pl.pallas_call(
        functools.partial(_pool_kernel, tm=tm, tiles_per_seq=seq_len // tm),
        out_shape=jax.ShapeDtypeStruct((T, width), BF16),
        grid=(T // tm,),
        in_specs=[pl.BlockSpec((tm, width), lambda i: (i, ublk)),
                  pl.BlockSpec((POOL_HALO, width),
                               lambda i: (jnp.maximum(i * rows_per_tile - 1, 0), ublk)),
                  pl.BlockSpec((None,) + w_grp.shape[1:], lambda i: (layer, 0, 0, 0)),
                  pl.BlockSpec((1, width), lambda i: (0, 0))],
        out_specs=pl.BlockSpec((tm, width), lambda i: (i, 0)),
        scratch_shapes=[pltpu.VMEM((tm + POOL_HALO, width), F32)],
        compiler_params=_params("parallel"),
        name="pool_mixer",
    )(proj, proj, w_grp, scale.reshape(1, width))


def _bucket_range(d_lo, d_hi):
    def bucket(d):
        if d < REL_MAX_EXACT:
            return d
        big = REL_MAX_EXACT + int(math.log(d / REL_MAX_EXACT) / math.log(REL_MAX_DIST / REL_MAX_EXACT)
                                  * (REL_BUCKETS - REL_MAX_EXACT))
        return min(big, REL_BUCKETS - 1)
    return max(bucket(max(d_lo, 0)) - 1, 0), min(bucket(d_hi) + 1, REL_BUCKETS - 1)


def _bias_tiles_kernel(tab_ref, o_ref, *, heads, blk, ndist):
    key = lax.broadcasted_iota(jnp.int32, (blk, blk), 0)
    qry = lax.broadcasted_iota(jnp.int32, (blk, blk), 1)
    for dist in range(ndist):
        @pl.when(pl.program_id(0) == dist)
        def _(dist=dist):
            d = dist * blk + qry - key
            n = jnp.maximum(d, 0)
            nf = jnp.maximum(n, 1).astype(F32)
            large = REL_MAX_EXACT + (jnp.log(nf / REL_MAX_EXACT) / math.log(REL_MAX_DIST / REL_MAX_EXACT)
                                     * (REL_BUCKETS - REL_MAX_EXACT)).astype(jnp.int32)
            large = jnp.minimum(large, REL_BUCKETS - 1)
            bucket = jnp.where(n < REL_MAX_EXACT, n, large)
            b_lo, b_hi = _bucket_range(dist * blk - (blk - 1), dist * blk + (blk - 1))
            masks = [bucket == b for b in range(b_lo, b_hi)]
            for h in range(heads):
                acc = jnp.full((blk, blk), tab_ref[h * REL_BUCKETS + b_hi] * LOG2E, F32)
                for b, mask in zip(range(b_lo, b_hi), masks):
                    acc = jnp.where(mask, tab_ref[h * REL_BUCKETS + b] * LOG2E, acc)
                o_ref[h, 0] = jnp.where(d >= 0, acc, NEG) if dist == 0 else acc


def rel_bias_tiles(rel_table, ndist, blk):
    heads = rel_table.shape[0]
    return pl.pallas_call(
        functools.partial(_bias_tiles_kernel, heads=heads, blk=blk, ndist=ndist),
        out_shape=jax.ShapeDtypeStruct((heads, ndist, blk, blk), F32),
        grid=(ndist,),
        in_specs=[pl.BlockSpec(memory_space=pltpu.SMEM)],
        out_specs=pl.BlockSpec((heads, 1, blk, blk), lambda d: (0, d, 0, 0)),
        compiler_params=_params("parallel"),
        name="rel_bias_tiles",
    )(rel_table.reshape(-1))


MOBA_VT_ROWS = HEAD_DIM + 8


def _moba_kernel(q_ref, k_ref, v_ref, bias_ref, o_ref, kaug_ref, vt_ref, kbar_ref, s_ref, *,
                 nb, blk, topk, scale):
    qi = pl.program_id(2)
    nstream, tq, dh = q_ref.shape
    seq = k_ref.shape[1]
    nbp = kbar_ref.shape[1]

    @pl.when(qi == 0)
    def _():
        row_blk = lax.broadcasted_iota(jnp.int32, (seq, dh), 0) // blk
        col = lax.broadcasted_iota(jnp.int32, (seq, dh), 1)
        onehot = (row_blk == col).astype(BF16)
        ones_rows = (lax.broadcasted_iota(jnp.int32, (MOBA_VT_ROWS - dh, tq), 0) == 0).astype(BF16)
        for r in range(nstream):
            kaug_ref[r, :, 0:dh] = (k_ref[r].astype(F32) * (scale * LOG2E)).astype(BF16)
            kaug_ref[r, :, dh:] = onehot
            for t in range(seq // tq):
                vt_ref[r, t, 0:dh, :] = v_ref[r, t * tq:(t + 1) * tq, :].astype(F32).T.astype(BF16)
                vt_ref[r, t, dh:, :] = ones_rows
            kbar_ref[r] = jnp.zeros(kbar_ref.shape[1:], F32)
            for n in range(nb):
                kblk = k_ref[r, n * blk:(n + 1) * blk, :].astype(F32)
                kbar_ref[r, n:n + 1, :] = jnp.mean(kblk, axis=0, keepdims=True)

    blk_id = lax.broadcasted_iota(jnp.int32, (nbp, tq), 0)
    own = qi * (tq // blk) + lax.broadcasted_iota(jnp.int32, (nbp, tq), 1) // blk
    past = blk_id < own

    def augmented_query(r):
        q = q_ref[r]
        kbar = kbar_ref[r]
        kbar_hi = kbar.astype(BF16)
        kbar_lo = (kbar - kbar_hi.astype(F32)).astype(BF16)
        gate = (lax.dot_general(kbar_hi, q, _NT, preferred_element_type=F32)
                + lax.dot_general(kbar_lo, q, _NT, preferred_element_type=F32))
        gate = jnp.where(past, gate, -jnp.inf)
        rank = jnp.zeros((nbp, tq), jnp.int32)
        for m in range(nb - 1):
            gm = gate[m:m + 1, :]
            before = (gm > gate) | ((gm == gate) & (m < blk_id))
            rank = rank + before.astype(jnp.int32)
        chosen = (past & (rank < topk)) | (blk_id == own)
        pen_t = jnp.where(chosen, 0.0, NEG)
        pen_t = jnp.concatenate([pen_t, jnp.zeros((dh - nbp, tq), F32)], axis=0)
        return jnp.concatenate([q, pen_t.T.astype(BF16)], axis=1)

    q_aug = [augmented_query(r) for r in range(nstream)]

    def produce(r, dist):
        start = pl.multiple_of((qi - dist) * tq, tq)
        s = lax.dot_general(kaug_ref[r, pl.ds(start, tq), :], q_aug[r], _NT,
                            preferred_element_type=F32) + bias_ref[dist]
        s_ref[dist % 2, r] = s
        return jnp.max(s, axis=0, keepdims=True)

    def absorb(r, dist, s_max, m_i, acc):
        m_new = jnp.maximum(m_i, s_max)
        p = jnp.exp2(s_ref[dist % 2, r] - m_new)
        acc = (jnp.exp2(m_i - m_new) * acc
               + jnp.dot(vt_ref[r, qi - dist], p.astype(BF16), preferred_element_type=F32))
        return m_new, acc

    def body(dist, carry):
        done = [absorb(r, dist, *carry[r]) for r in range(nstream)]
        return tuple((produce(r, dist + 1),) + done[r] for r in range(nstream))

    init = tuple((produce(r, 0), jnp.full((1, tq), NEG, F32), jnp.zeros((MOBA_VT_ROWS, tq), F32))
                 for r in range(nstream))
    last = lax.fori_loop(0, qi, body, init)
    for r in range(nstream):
        _, acc = absorb(r, qi, *last[r])
        o_ref[r] = (acc[0:dh] / acc[dh:dh + 1]).T.astype(o_ref.dtype)


def moba_attention(proj3, q_off, k_off, v_off, bias_tiles, heads, *, nstream=4):
    B, S, _ = proj3.shape
    blk, dh = MOBA_BLOCK, HEAD_DIM
    tq = bias_tiles.shape[-1]
    assert S % tq == 0 and tq % blk == 0 and q_off % dh == 0 and k_off % dh == 0 and v_off % dh == 0
    nstream = nstream if B % nstream == 0 else 1
    nb = S // blk
    nbp = -(-nb // 8) * 8
    assert nbp <= dh, "block one-hot columns must fit beside the keys"
    topk = min(MOBA_TOPK, nb)
    qb, kb_, vb_ = q_off // dh, k_off // dh, v_off // dh
    kern = functools.partial(_moba_kernel, nb=nb, blk=blk, topk=topk, scale=dh ** -0.5)
    return pl.pallas_call(
        kern,
        out_shape=jax.ShapeDtypeStruct((B, S, heads * dh), BF16),
        grid=(heads, B // nstream, S // tq),
        in_specs=[pl.BlockSpec((nstream, tq, dh), lambda h, b, i: (b, i, qb + h)),
                  pl.BlockSpec((nstream, S, dh), lambda h, b, i: (b, 0, kb_ + h),
                               pipeline_mode=pl.Buffered(1)),
                  pl.BlockSpec((nstream, S, dh), lambda h, b, i: (b, 0, vb_ + h),
                               pipeline_mode=pl.Buffered(1)),
                  pl.BlockSpec((None, S // tq, tq, tq), lambda h, b, i: (h, 0, 0, 0),
                               pipeline_mode=pl.Buffered(1))],
        out_specs=pl.BlockSpec((nstream, tq, dh), lambda h, b, i: (b, i, h)),
        scratch_shapes=[pltpu.VMEM((nstream, S, 2 * dh), BF16),
                        pltpu.VMEM((nstream, S // tq, MOBA_VT_ROWS, tq), BF16),
                        pltpu.VMEM((nstream, nbp, dh), F32),
                        pltpu.VMEM((2, nstream, tq, tq), F32)],
        compiler_params=_params("parallel", "parallel", "arbitrary"),
        name="moba_attention",
    )(proj3, proj3, proj3, bias_tiles)


def _suffix_sum_matrix(bk):
    r = lax.broadcasted_iota(jnp.int32, (2 * bk, 2 * bk), 0) % bk
    c = lax.broadcasted_iota(jnp.int32, (2 * bk, 2 * bk), 1)
    return ((c >= bk) | (r > c)).astype(BF16)


def _sb_kernel(q_ref, k_ref, v_ref, o_ref, *, scale):
    qi = pl.program_id(2)
    nstream, tq, dh = q_ref.shape
    half = tq // 2
    sums = _suffix_sum_matrix(half)
    q2 = [(q_ref[r].astype(F32) * (scale * LOG2E)).astype(BF16) for r in range(nstream)]

    def tile(r, j, run, acc, valid):
        start = pl.multiple_of(j * tq, tq)
        z = lax.dot_general(q2[r], k_ref[r, pl.ds(start, tq), :], _NT, preferred_element_type=F32)
        sp = jnp.maximum(z, 0.0) + jnp.log2(1.0 + jnp.exp2(-jnp.abs(z)))
        log_1m = -sp
        if valid is not None:
            log_1m = jnp.where(valid, log_1m, 0.0)
        hi = log_1m.astype(BF16)
        lo = (log_1m - hi.astype(F32)).astype(BF16)
        cs_b = jnp.dot(jnp.concatenate([hi[:, half:], lo[:, half:]], axis=1), sums,
                       preferred_element_type=F32)
        cs_a = jnp.dot(jnp.concatenate([hi[:, :half], lo[:, :half]], axis=1), sums,
                       preferred_element_type=F32)
        run_b = run + cs_b[:, half:]
        after = jnp.concatenate([run_b + cs_a[:, :half], run + cs_b[:, :half]], axis=1)
        a = jnp.exp2((z - sp) + after)
        if valid is not None:
            a = jnp.where(valid, a, 0.0)
        acc = acc + jnp.dot(a.astype(BF16), v_ref[r, pl.ds(start, tq), :], preferred_element_type=F32)
        return run_b + cs_a[:, half:], acc

    row = lax.broadcasted_iota(jnp.int32, (tq, tq), 0)
    col = lax.broadcasted_iota(jnp.int32, (tq, tq), 1)
    no_prev = jnp.where(qi > 0, 0.0, 4.0 * EXP2_ZERO_BELOW)
    state = []
    for r in range(nstream):
        run, acc = tile(r, qi, jnp.zeros((tq, half), F32), jnp.zeros((tq, dh), F32), col < row)
        state.append(tile(r, jnp.maximum(qi - 1, 0), run + no_prev, acc, None))

    def run_max(state):
        return functools.reduce(jnp.maximum, [jnp.max(run) for run, _ in state])

    def cond(carry):
        j, _, top = carry
        return (j >= 0) & (top > EXP2_ZERO_BELOW)

    def body(carry):
        j, state, _ = carry
        state = tuple(tile(r, j, run, acc, None) for r, (run, acc) in enumerate(state))
        return j - 1, state, run_max(state)

    _, state, _ = lax.while_loop(cond, body, (qi - 2, tuple(state), run_max(state)))
    for r, (_, acc) in enumerate(state):
        o_ref[r] = acc.astype(o_ref.dtype)


def stick_breaking_attention(proj3, q_off, k_off, v_off, heads, *, tq=2 * V7X_LANES, nstream=4):
    B, S, _ = proj3.shape
    dh = HEAD_DIM
    assert tq == 2 * V7X_LANES, "suffix sums work on 128-key halves of a tile"
    assert S % tq == 0 and q_off % dh == 0 and k_off % dh == 0 and v_off % dh == 0
    nstream = nstream if B % nstream == 0 else 1
    qb, kb_, vb_ = q_off // dh, k_off // dh, v_off // dh
    return pl.pallas_call(
        functools.partial(_sb_kernel, scale=dh ** -0.5),
        out_shape=jax.ShapeDtypeStruct((B, S, heads * dh), BF16),
        grid=(heads, B // nstream, S // tq),
        in_specs=[pl.BlockSpec((nstream, tq, dh), lambda h, b, i: (b, i, qb + h)),
                  pl.BlockSpec((nstream, S, dh), lambda h, b, i: (b, 0, kb_ + h)),
                  pl.BlockSpec((nstream, S, dh), lambda h, b, i: (b, 0, vb_ + h))],
        out_specs=pl.BlockSpec((nstream, tq, dh), lambda h, b, i: (b, i, h)),
        compiler_params=_params("parallel", "parallel", "parallel"),
        name="stick_breaking_attention",
    )(proj3, proj3, proj3)


def kernel(x, norm_mix, norm_ffn, w_in, w_pool, pool_scale, w_br_a, w_br_b, w_br_c, w_out, w_gate,
           w_up, w_down, rel_bias, norm_final):
    B, S, D = x.shape
    depth = w_in.shape[0]
    T = B * S
    moba_w = MOBA_HEADS * HEAD_DIM
    sb_w = SB_HEADS * HEAD_DIM
    pool_w = len(POOL_WINDOWS) * POOL_GROUP
    off_qa, off_ka, off_va = 0, moba_w, 2 * moba_w
    off_u = 3 * moba_w
    off_qs = off_u + pool_w
    off_ks, off_vs = off_qs + sb_w, off_qs + 2 * sb_w
    off_gates = off_qs + 3 * sb_w

    bias_tiles = rel_bias_tiles(rel_bias, S // MOBA_QTILE, MOBA_QTILE)
    w_pool, w_br_a, w_br_b, w_br_c, w_out, w_down = (
        w.astype(BF16) for w in (w_pool, w_br_a, w_br_b, w_br_c, w_out, w_down))
    h = x.reshape(T, D)
    for l in range(depth):
        proj = norm_matmul(h, norm_mix[l], w_in, l)
        proj3 = proj.reshape(B, S, proj.shape[1])
        ya = moba_attention(proj3, off_qa, off_ka, off_va, bias_tiles, MOBA_HEADS).reshape(T, moba_w)
        yb = pool_mixer(proj, off_u, w_pool, l, pool_scale[l], S)
        yc = stick_breaking_attention(proj3, off_qs, off_ks, off_vs, SB_HEADS).reshape(T, sb_w)
        m = merge_branches(ya, yb, yc, proj, off_gates, w_br_a, w_br_b, w_br_c, l)
        h = matmul_res(m, w_out, l, h, tn=1024)
        act = ffn_up(h, norm_ffn[l], w_gate, w_up, l)
        h = matmul_res(act, w_down, l, h)
    return final_norm(h, norm_final).reshape(B, S, D)
```

```python
import functools
import math

import jax
import jax.numpy as jnp
from jax import lax
from jax.experimental import pallas as pl
from jax.experimental.pallas import tpu as pltpu

F32 = jnp.float32
BF16 = jnp.bfloat16

HEAD_DIM = 128
MOBA_HEADS = 8
MOBA_BLOCK = 256
MOBA_TOPK = 3
MOBA_QTILE = 512
SB_HEADS = 8
POOL_WINDOWS = (2, 4, 8, 16)
POOL_GROUP = 256
POOL_HALO = 16
REL_BUCKETS = 32
REL_MAX_EXACT = 16
REL_MAX_DIST = 2048
EPS = 1e-6
NEG = -1e30
LOG2E = math.log2(math.e)
EXP2_ZERO_BELOW = -152.0

V7X_LANES = 128
V7X_VMEM_LIMIT_BYTES = 56 * 1024 * 1024

_NT = (((1,), (1,)), ((), ()))


def _params(*semantics):
    return pltpu.CompilerParams(dimension_semantics=semantics,
                                vmem_limit_bytes=V7X_VMEM_LIMIT_BYTES)


def _pick_tile(n, want, align):
    t = min(want, n)
    while t > align and (n % t or t % align):
        t -= align
    assert n % t == 0 and t % align == 0, (n, want, align)
    return t


def _rms_normalize(x, g):
    ms = jnp.mean(x * x, axis=-1, keepdims=True)
    return x * lax.rsqrt(ms + EPS) * g


NORM_CHUNK = 256


def _norm_matmul_kernel(x_ref, g_ref, w_ref, o_ref, xn_ref):
    @pl.when(pl.program_id(1) == 0)
    def _():
        w = w_ref[...].astype(BF16)
        for r0 in range(0, x_ref.shape[0], NORM_CHUNK):
            rows = pl.ds(r0, NORM_CHUNK)
            xn = _rms_normalize(x_ref[rows, :], g_ref[...]).astype(BF16)
            xn_ref[rows, :] = xn
            o_ref[rows, :] = jnp.dot(xn, w, preferred_element_type=F32).astype(o_ref.dtype)

    @pl.when(pl.program_id(1) > 0)
    def _():
        o_ref[...] = jnp.dot(xn_ref[...], w_ref[...].astype(BF16),
                             preferred_element_type=F32).astype(o_ref.dtype)


def norm_matmul(x, g, w, layer, *, out_dtype=BF16, tm=1024, tn=1024):
    T, K = x.shape
    N = w.shape[2]
    tm = _pick_tile(T, tm, 16)
    tn = _pick_tile(N, tn, V7X_LANES)
    return pl.pallas_call(
        _norm_matmul_kernel,
        out_shape=jax.ShapeDtypeStruct((T, N), out_dtype),
        grid=(T // tm, N // tn),
        in_specs=[pl.BlockSpec((tm, K), lambda i, j: (i, 0)),
                  pl.BlockSpec((1, K), lambda i, j: (0, 0)),
                  pl.BlockSpec((None, K, tn), lambda i, j: (layer, 0, j))],
        out_specs=pl.BlockSpec((tm, tn), lambda i, j: (i, j)),
        scratch_shapes=[pltpu.VMEM((tm, K), BF16)],
        compiler_params=_params("parallel", "arbitrary"),
        name="norm_matmul",
    )(x, g.reshape(1, K), w)


def _ffn_up_kernel(x_ref, g_ref, wg_ref, wu_ref, o_ref, xn_ref):
    def swiglu(xn, wg, wu):
        a = jnp.dot(xn, wg, preferred_element_type=F32)
        b = jnp.dot(xn, wu, preferred_element_type=F32)
        return (a * jax.nn.sigmoid(a) * b).astype(o_ref.dtype)

    @pl.when(pl.program_id(1) == 0)
    def _():
        wg, wu = wg_ref[...].astype(BF16), wu_ref[...].astype(BF16)
        for r0 in range(0, x_ref.shape[0], NORM_CHUNK):
            rows = pl.ds(r0, NORM_CHUNK)
            xn = _rms_normalize(x_ref[rows, :], g_ref[...]).astype(BF16)
            xn_ref[rows, :] = xn
            o_ref[rows, :] = swiglu(xn, wg, wu)

    @pl.when(pl.program_id(1) > 0)
    def _():
        o_ref[...] = swiglu(xn_ref[...], wg_ref[...].astype(BF16), wu_ref[...].astype(BF16))


def ffn_up(x, g, wg, wu, layer, *, tm=1024, tn=512):
    T, K = x.shape
    N = wg.shape[2]
    tm = _pick_tile(T, tm, 16)
    tn = _pick_tile(N, tn, V7X_LANES)
    return pl.pallas_call(
        _ffn_up_kernel,
        out_shape=jax.ShapeDtypeStruct((T, N), BF16),
        grid=(T // tm, N // tn),
        in_specs=[pl.BlockSpec((tm, K), lambda i, j: (i, 0)),
                  pl.BlockSpec((1, K), lambda i, j: (0, 0)),
                  pl.BlockSpec((None, K, tn), lambda i, j: (layer, 0, j)),
                  pl.BlockSpec((None, K, tn), lambda i, j: (layer, 0, j))],
        out_specs=pl.BlockSpec((tm, tn), lambda i, j: (i, j)),
        scratch_shapes=[pltpu.VMEM((tm, K), BF16)],
        compiler_params=_params("parallel", "arbitrary"),
        name="ffn_up",
    )(x, g.reshape(1, K), wg, wu)


def _matmul_res_kernel(x_ref, w_ref, r_ref, o_ref):
    o_ref[...] = r_ref[...] + jnp.dot(x_ref[...], w_ref[...], preferred_element_type=F32)


def matmul_res(x, w, layer, res, *, tm=1024, tn=512):
    T, K = x.shape
    N = w.shape[2]
    tm = _pick_tile(T, tm, 16)
    tn = _pick_tile(N, tn, V7X_LANES)
    return pl.pallas_call(
        _matmul_res_kernel,
        out_shape=jax.ShapeDtypeStruct((T, N), F32),
        grid=(T // tm, N // tn),
        in_specs=[pl.BlockSpec((tm, K), lambda i, j: (i, 0)),
                  pl.BlockSpec((None, K, tn), lambda i, j: (layer, 0, j)),
                  pl.BlockSpec((tm, tn), lambda i, j: (i, j))],
        out_specs=pl.BlockSpec((tm, tn), lambda i, j: (i, j)),
        compiler_params=_params("parallel", "parallel"),
        name="matmul_res",
    )(x, w, res)


def _merge_kernel(ya_ref, yb_ref, yc_ref, ga_ref, gb_ref, gc_ref, wa_ref, wb_ref, wc_ref, o_ref):
    def branch(y_ref, g_ref, w_ref):
        gate = jax.nn.sigmoid(g_ref[...].astype(F32))
        return gate * jnp.dot(y_ref[...], w_ref[...], preferred_element_type=F32)

    m = branch(ya_ref, ga_ref, wa_ref) + branch(yb_ref, gb_ref, wb_ref) + branch(yc_ref, gc_ref, wc_ref)
    o_ref[...] = m.astype(o_ref.dtype)


def merge_branches(ya, yb, yc, proj, gate_off, wa, wb, wc, layer, *, tm=1024, tn=1024):
    T, K = ya.shape
    N = wa.shape[2]
    tm = _pick_tile(T, tm, 16)
    tn = _pick_tile(N, tn, V7X_LANES)
    assert gate_off % tn == 0
    y_spec = pl.BlockSpec((tm, K), lambda i, j: (i, 0))
    w_spec = pl.BlockSpec((None, K, tn), lambda i, j: (layer, 0, j))

    def gate_spec(branch):
        base = (gate_off + branch * N) // tn
        return pl.BlockSpec((tm, tn), lambda i, j: (i, base + j))

    return pl.pallas_call(
        _merge_kernel,
        out_shape=jax.ShapeDtypeStruct((T, N), BF16),
        grid=(T // tm, N // tn),
        in_specs=[y_spec, y_spec, y_spec, gate_spec(0), gate_spec(1), gate_spec(2),
                  w_spec, w_spec, w_spec],
        out_specs=pl.BlockSpec((tm, tn), lambda i, j: (i, j)),
        compiler_params=_params("parallel", "parallel"),
        name="merge_branches",
    )(ya, yb, yc, proj, proj, proj, wa, wb, wc)


def _final_norm_kernel(x_ref, g_ref, o_ref):
    o_ref[...] = _rms_normalize(x_ref[...], g_ref[...])


def final_norm(x, g, *, tm=512):
    T, K = x.shape
    tm = _pick_tile(T, tm, 8)
    return pl.pallas_call(
        _final_norm_kernel,
        out_shape=jax.ShapeDtypeStruct((T, K), F32),
        grid=(T // tm,),
        in_specs=[pl.BlockSpec((tm, K), lambda i: (i, 0)),
                  pl.BlockSpec((1, K), lambda i: (0, 0))],
        out_specs=pl.BlockSpec((tm, K), lambda i: (i, 0)),
        compiler_params=_params("parallel"),
        name="final_norm",
    )(x, g.reshape(1, K))


POOL_PAD = 8


def _pool_kernel(u_ref, halo_ref, w_ref, sc_ref, o_ref, ext_ref, stage_ref, *, tm, tiles_per_seq):
    t_in_seq = pl.program_id(0) % tiles_per_seq
    base = POOL_PAD + POOL_HALO
    rows = tm + POOL_HALO
    zeros = jnp.zeros((POOL_PAD, ext_ref.shape[1]), F32)
    ext_ref[0:POOL_PAD, :] = zeros
    ext_ref[POOL_PAD:base, :] = jnp.where(t_in_seq == 0, 0.0, halo_ref[...].astype(F32))
    ext_ref[base:, :] = u_ref[...].astype(F32)
    stage_ref[0:POOL_PAD, :] = zeros[:, 0:POOL_GROUP]
    pos = t_in_seq * tm + lax.broadcasted_iota(jnp.int32, (tm, 1), 0)
    for gi, win in enumerate(POOL_WINDOWS):
        cols = slice(gi * POOL_GROUP, (gi + 1) * POOL_GROUP)
        x = ext_ref[base:, cols]
        cur = ext_ref[pl.ds(POOL_PAD, rows), cols] + ext_ref[pl.ds(POOL_PAD - 1, rows), cols]
        span = 2
        while span < win:
            stage_ref[pl.ds(POOL_PAD, rows), :] = cur
            cur = cur + stage_ref[pl.ds(POOL_PAD - span, rows), :]
            span *= 2
        assert span == win <= 2 * POOL_PAD, "pooling windows must be powers of two up to 2 * POOL_PAD"
        total = cur[POOL_HALO:]
        cnt = jnp.minimum(pos + 1, win).astype(F32)
        pooled = total / cnt - x
        y = jnp.dot(pooled.astype(BF16), w_ref[gi], preferred_element_type=F32)
        o_ref[:, cols] = (y * sc_ref[:, cols]).astype(o_ref.dtype)


def pool_mixer(proj, u_off, w_grp, layer, scale, seq_len, *, tm=1024):
    T = proj.shape[0]
    width = len(POOL_WINDOWS) * POOL_GROUP
    tm = _pick_tile(seq_len, tm, POOL_HALO)
    assert u_off % width == 0 and max(POOL_WINDOWS) <= POOL_HALO
    ublk = u_off // width
    rows_per_tile = tm // POOL_HALO
    return pl.pallas_call(
        functools.partial(_pool_kernel, tm=tm, tiles_per_seq=seq_len // tm),
        out_shape=jax.ShapeDtypeStruct((T, width), BF16),
        grid=(T // tm,),
        in_specs=[pl.BlockSpec((tm, width), lambda i: (i, ublk)),
                  pl.BlockSpec((POOL_HALO, width),
                               lambda i: (jnp.maximum(i * rows_per_tile - 1, 0), ublk)),
                  pl.BlockSpec((None,) + w_grp.shape[1:], lambda i: (layer, 0, 0, 0)),
                  pl.BlockSpec((1, width), lambda i: (0, 0))],
        out_specs=pl.BlockSpec((tm, width), lambda i: (i, 0)),
        scratch_shapes=[pltpu.VMEM((POOL_PAD + POOL_HALO + tm, width), F32),
                        pltpu.VMEM((POOL_PAD + POOL_HALO + tm, POOL_GROUP), F32)],
        compiler_params=_params("parallel"),
        name="pool_mixer",
    )(proj, proj, w_grp, scale.reshape(1, width))


def _bucket_range(d_lo, d_hi):
    def bucket(d):
        if d < REL_MAX_EXACT:
            return d
        big = REL_MAX_EXACT + int(math.log(d / REL_MAX_EXACT) / math.log(REL_MAX_DIST / REL_MAX_EXACT)
                                  * (REL_BUCKETS - REL_MAX_EXACT))
        return min(big, REL_BUCKETS - 1)
    return max(bucket(max(d_lo, 0)) - 1, 0), min(bucket(d_hi) + 1, REL_BUCKETS - 1)


BIAS_ROWS = 8


def _bias_tiles_kernel(tab_ref, o_ref, *, heads, blk, ndist):
    key0 = lax.broadcasted_iota(jnp.int32, (BIAS_ROWS, blk), 0)
    qry = lax.broadcasted_iota(jnp.int32, (BIAS_ROWS, blk), 1)
    for dist in range(ndist):
        @pl.when(pl.program_id(0) == dist)
        def _(dist=dist):
            b_lo, b_hi = _bucket_range(dist * blk - (blk - 1), dist * blk + (blk - 1))
            table = [[tab_ref[h * REL_BUCKETS + b] * LOG2E for b in range(b_lo, b_hi + 1)]
                     for h in range(heads)]

            def rows(c, carry):
                r0 = pl.multiple_of(c * BIAS_ROWS, BIAS_ROWS)
                d = dist * blk + qry - (key0 + r0)
                n = jnp.maximum(d, 0)
                nf = jnp.maximum(n, 1).astype(F32)
                large = REL_MAX_EXACT + (jnp.log(nf / REL_MAX_EXACT)
                                         / math.log(REL_MAX_DIST / REL_MAX_EXACT)
                                         * (REL_BUCKETS - REL_MAX_EXACT)).astype(jnp.int32)
                large = jnp.minimum(large, REL_BUCKETS - 1)
                bucket = jnp.where(n < REL_MAX_EXACT, n, large)
                accs = [jnp.full((BIAS_ROWS, blk), table[h][-1], F32) for h in range(heads)]
                for i, b in enumerate(range(b_lo, b_hi)):
                    mask = bucket == b
                    accs = [jnp.where(mask, table[h][i], accs[h]) for h in range(heads)]
                for h in range(heads):
                    val = jnp.where(d >= 0, accs[h], NEG) if dist == 0 else accs[h]
                    o_ref[h, 0, pl.ds(r0, BIAS_ROWS), :] = val
                return carry

            lax.fori_loop(0, blk // BIAS_ROWS, rows, 0)


def rel_bias_tiles(rel_table, ndist, blk):
    heads = rel_table.shape[0]
    return pl.pallas_call(
        functools.partial(_bias_tiles_kernel, heads=heads, blk=blk, ndist=ndist),
        out_shape=jax.ShapeDtypeStruct((heads, ndist, blk, blk), F32),
        grid=(ndist,),
        in_specs=[pl.BlockSpec(memory_space=pltpu.SMEM)],
        out_specs=pl.BlockSpec((heads, 1, blk, blk), lambda d: (0, d, 0, 0)),
        compiler_params=_params("parallel"),
        name="rel_bias_tiles",
    )(rel_table.reshape(-1))


MOBA_VT_ROWS = HEAD_DIM + 8


def _moba_kernel(q_ref, k_ref, v_ref, bias_ref, o_ref, kaug_ref, vt_ref, kbar_ref, s_ref, *,
                 nb, blk, topk, scale):
    qi = pl.program_id(2)
    nstream, tq, dh = q_ref.shape
    seq = k_ref.shape[1]
    nbp = kbar_ref.shape[1]

    @pl.when(qi == 0)
    def _():
        row_blk = lax.broadcasted_iota(jnp.int32, (seq, dh), 0) // blk
        col = lax.broadcasted_iota(jnp.int32, (seq, dh), 1)
        onehot = (row_blk == col).astype(BF16)
        ones_rows = (lax.broadcasted_iota(jnp.int32, (MOBA_VT_ROWS - dh, tq), 0) == 0).astype(BF16)
        for r in range(nstream):
            kaug_ref[r, :, 0:dh] = (k_ref[r].astype(F32) * (scale * LOG2E)).astype(BF16)
            kaug_ref[r, :, dh:] = onehot
            for t in range(seq // tq):
                vt_ref[r, t, 0:dh, :] = v_ref[r, t * tq:(t + 1) * tq, :].astype(F32).T.astype(BF16)
                vt_ref[r, t, dh:, :] = ones_rows
            kbar_ref[r] = jnp.zeros(kbar_ref.shape[1:], F32)
            for n in range(nb):
                kblk = k_ref[r, n * blk:(n + 1) * blk, :].astype(F32)
                kbar_ref[r, n:n + 1, :] = jnp.mean(kblk, axis=0, keepdims=True)

    blk_id = lax.broadcasted_iota(jnp.int32, (nbp, tq), 0)
    own = qi * (tq // blk) + lax.broadcasted_iota(jnp.int32, (nbp, tq), 1) // blk
    past = blk_id < own

    def augmented_query(r):
        q = q_ref[r]
        kbar = kbar_ref[r]
        kbar_hi = kbar.astype(BF16)
        kbar_lo = (kbar - kbar_hi.astype(F32)).astype(BF16)
        gate = (lax.dot_general(kbar_hi, q, _NT, preferred_element_type=F32)
                + lax.dot_general(kbar_lo, q, _NT, preferred_element_type=F32))
        gate = jnp.where(past, gate, -jnp.inf)
        rank = jnp.zeros((nbp, tq), jnp.int32)
        for m in range(nb - 1):
            gm = gate[m:m + 1, :]
            before = (gm > gate) | ((gm == gate) & (m < blk_id))
            rank = rank + before.astype(jnp.int32)
        chosen = (past & (rank < topk)) | (blk_id == own)
        pen_t = jnp.where(chosen, 0.0, NEG)
        pen_t = jnp.concatenate([pen_t, jnp.zeros((dh - nbp, tq), F32)], axis=0)
        return jnp.concatenate([q, pen_t.T.astype(BF16)], axis=1)

    q_aug = [augmented_query(r) for r in range(nstream)]

    def produce(r, dist):
        start = pl.multiple_of((qi - dist) * tq, tq)
        s = lax.dot_general(kaug_ref[r, pl.ds(start, tq), :], q_aug[r], _NT,
                            preferred_element_type=F32) + bias_ref[dist]
        s_ref[dist % 2, r] = s
        return jnp.max(s, axis=0, keepdims=True)

    def absorb(r, dist, s_max, m_i, acc):
        m_new = jnp.maximum(m_i, s_max)
        p = jnp.exp2(s_ref[dist % 2, r] - m_new)
        acc = (jnp.exp2(m_i - m_new) * acc
               + jnp.dot(vt_ref[r, qi - dist], p.astype(BF16), preferred_element_type=F32))
        return m_new, acc

    def body(dist, carry):
        done = [absorb(r, dist, *carry[r]) for r in range(nstream)]
        return tuple((produce(r, dist + 1),) + done[r] for r in range(nstream))

    init = tuple((produce(r, 0), jnp.full((1, tq), NEG, F32), jnp.zeros((MOBA_VT_ROWS, tq), F32))
                 for r in range(nstream))
    last = lax.fori_loop(0, qi, body, init)
    for r in range(nstream):
        _, acc = absorb(r, qi, *last[r])
        o_ref[r] = (acc[0:dh] / acc[dh:dh + 1]).T.astype(o_ref.dtype)


def moba_attention(proj3, q_off, k_off, v_off, bias_tiles, heads, *, nstream=4):
    B, S, _ = proj3.shape
    blk, dh = MOBA_BLOCK, HEAD_DIM
    tq = bias_tiles.shape[-1]
    assert S % tq == 0 and tq % blk == 0 and q_off % dh == 0 and k_off % dh == 0 and v_off % dh == 0
    nstream = nstream if B % nstream == 0 else 1
    nb = S // blk
    nbp = -(-nb // 8) * 8
    assert nbp <= dh, "block one-hot columns must fit beside the keys"
    topk = min(MOBA_TOPK, nb)
    qb, kb_, vb_ = q_off // dh, k_off // dh, v_off // dh
    kern = functools.partial(_moba_kernel, nb=nb, blk=blk, topk=topk, scale=dh ** -0.5)
    return pl.pallas_call(
        kern,
        out_shape=jax.ShapeDtypeStruct((B, S, heads * dh), BF16),
        grid=(heads, B // nstream, S // tq),
        in_specs=[pl.BlockSpec((nstream, tq, dh), lambda h, b, i: (b, i, qb + h)),
                  pl.BlockSpec((nstream, S, dh), lambda h, b, i: (b, 0, kb_ + h),
                               pipeline_mode=pl.Buffered(1)),
                  pl.BlockSpec((nstream, S, dh), lambda h, b, i: (b, 0, vb_ + h),
                               pipeline_mode=pl.Buffered(1)),
                  pl.BlockSpec((None, S // tq, tq, tq), lambda h, b, i: (h, 0, 0, 0),
                               pipeline_mode=pl.Buffered(1))],
        out_specs=pl.BlockSpec((nstream, tq, dh), lambda h, b, i: (b, i, h)),
        scratch_shapes=[pltpu.VMEM((nstream, S, 2 * dh), BF16),
                        pltpu.VMEM((nstream, S // tq, MOBA_VT_ROWS, tq), BF16),
                        pltpu.VMEM((nstream, nbp, dh), F32),
                        pltpu.VMEM((2, nstream, tq, tq), F32)],
        compiler_params=_params("parallel", "parallel", "arbitrary"),
        name="moba_attention",
    )(proj3, proj3, proj3, bias_tiles)


def _suffix_sum_matrix(bk):
    r = lax.broadcasted_iota(jnp.int32, (2 * bk, 2 * bk), 0) % bk
    c = lax.broadcasted_iota(jnp.int32, (2 * bk, 2 * bk), 1)
    return ((c >= bk) | (r > c)).astype(BF16)


def _sb_kernel(q_ref, k_ref, v_ref, o_ref, *, scale):
    qi = pl.program_id(2)
    nbatch, tq, width = q_ref.shape
    dh = HEAD_DIM
    half = tq // 2
    sums = _suffix_sum_matrix(half)
    streams = [(r, c0) for r in range(nbatch) for c0 in range(0, width, dh)]
    nstream = len(streams)
    q2 = [(q_ref[r, :, c0:c0 + dh].astype(F32) * (scale * LOG2E)).astype(BF16) for r, c0 in streams]

    def tile(s, j, run, acc, valid):
        r, c0 = streams[s]
        start = pl.multiple_of(j * tq, tq)
        z = lax.dot_general(q2[s], k_ref[r, pl.ds(start, tq), c0:c0 + dh], _NT,
                            preferred_element_type=F32)
        sp = jnp.maximum(z, 0.0) + jnp.log2(1.0 + jnp.exp2(-jnp.abs(z)))
        log_1m = -sp
        if valid is not None:
            log_1m = jnp.where(valid, log_1m, 0.0)
        hi = log_1m.astype(BF16)
        lo = (log_1m - hi.astype(F32)).astype(BF16)
        cs_b = jnp.dot(jnp.concatenate([hi[:, half:], lo[:, half:]], axis=1), sums,
                       preferred_element_type=F32)
        cs_a = jnp.dot(jnp.concatenate([hi[:, :half], lo[:, :half]], axis=1), sums,
                       preferred_element_type=F32)
        run_b = run + cs_b[:, half:]
        after = jnp.concatenate([run_b + cs_a[:, :half], run + cs_b[:, :half]], axis=1)
        a = jnp.exp2((z - sp) + after)
        if valid is not None:
            a = jnp.where(valid, a, 0.0)
        acc = acc + jnp.dot(a.astype(BF16), v_ref[r, pl.ds(start, tq), c0:c0 + dh],
                            preferred_element_type=F32)
        return run_b + cs_a[:, half:], acc

    row = lax.broadcasted_iota(jnp.int32, (tq, tq), 0)
    col = lax.broadcasted_iota(jnp.int32, (tq, tq), 1)
    no_prev = jnp.where(qi > 0, 0.0, 4.0 * EXP2_ZERO_BELOW)
    state = []
    for s in range(nstream):
        run, acc = tile(s, qi, jnp.zeros((tq, half), F32), jnp.zeros((tq, dh), F32), col < row)
        state.append(tile(s, jnp.maximum(qi - 1, 0), run + no_prev, acc, None))

    def run_max(state):
        return functools.reduce(jnp.maximum, [jnp.max(run) for run, _ in state])

    def cond(carry):
        j, _, top = carry
        return (j >= 0) & (top > EXP2_ZERO_BELOW)

    def body(carry):
        j, state, _ = carry
        state = tuple(tile(s, j, run, acc, None) for s, (run, acc) in enumerate(state))
        return j - 1, state, run_max(state)

    _, state, _ = lax.while_loop(cond, body, (qi - 2, tuple(state), run_max(state)))
    for (r, c0), (_, acc) in zip(streams, state):
        o_ref[r, :, c0:c0 + dh] = acc.astype(o_ref.dtype)


def stick_breaking_attention(proj3, q_off, k_off, v_off, heads, *, tq=2 * V7X_LANES, nbatch=4, nhead=2):
    B, S, _ = proj3.shape
    dh = HEAD_DIM
    assert tq == 2 * V7X_LANES, "suffix sums work on 128-key halves of a tile"
    nbatch = nbatch if B % nbatch == 0 else 1
    width = nhead * dh
    assert S % tq == 0 and heads % nhead == 0
    assert q_off % width == 0 and k_off % width == 0 and v_off % width == 0
    qb, kb_, vb_ = q_off // width, k_off // width, v_off // width
    return pl.pallas_call(
        functools.partial(_sb_kernel, scale=dh ** -0.5),
        out_shape=jax.ShapeDtypeStruct((B, S, heads * dh), BF16),
        grid=(heads // nhead, B // nbatch, S // tq),
        in_specs=[pl.BlockSpec((nbatch, tq, width), lambda h, b, i: (b, i, qb + h)),
                  pl.BlockSpec((nbatch, S, width), lambda h, b, i: (b, 0, kb_ + h)),
                  pl.BlockSpec((nbatch, S, width), lambda h, b, i: (b, 0, vb_ + h))],
        out_specs=pl.BlockSpec((nbatch, tq, width), lambda h, b, i: (b, i, h)),
        compiler_params=_params("parallel", "parallel", "parallel"),
        name="stick_breaking_attention",
    )(proj3, proj3, proj3)


def kernel(x, norm_mix, norm_ffn, w_in, w_pool, pool_scale, w_br_a, w_br_b, w_br_c, w_out, w_gate,
           w_up, w_down, rel_bias, norm_final):
    B, S, D = x.shape
    depth = w_in.shape[0]
    T = B * S
    moba_w = MOBA_HEADS * HEAD_DIM
    sb_w = SB_HEADS * HEAD_DIM
    pool_w = len(POOL_WINDOWS) * POOL_GROUP
    off_qa, off_ka, off_va = 0, moba_w, 2 * moba_w
    off_u = 3 * moba_w
    off_qs = off_u + pool_w
    off_ks, off_vs = off_qs + sb_w, off_qs + 2 * sb_w
    off_gates = off_qs + 3 * sb_w

    bias_tiles = rel_bias_tiles(rel_bias, S // MOBA_QTILE, MOBA_QTILE)
    w_pool, w_br_a, w_br_b, w_br_c, w_out, w_down = (
        w.astype(BF16) for w in (w_pool, w_br_a, w_br_b, w_br_c, w_out, w_down))
    h = x.reshape(T, D)
    for l in range(depth):
        proj = norm_matmul(h, norm_mix[l], w_in, l)
        proj3 = proj.reshape(B, S, proj.shape[1])
        ya = moba_attention(proj3, off_qa, off_ka, off_va, bias_tiles, MOBA_HEADS).reshape(T, moba_w)
        yb = pool_mixer(proj, off_u, w_pool, l, pool_scale[l], S)
        yc = stick_breaking_attention(proj3, off_qs, off_ks, off_vs, SB_HEADS).reshape(T, sb_w)
        m = merge_branches(ya, yb, yc, proj, off_gates, w_br_a, w_br_b, w_br_c, l)
        h = matmul_res(m, w_out, l, h, tn=1024)
        act = ffn_up(h, norm_ffn[l], w_gate, w_up, l)
        h = matmul_res(act, w_down, l, h)
    return final_norm(h, norm_final).reshape(B, S, D)
```

```python
import functools
import math

import jax
import jax.numpy as jnp
from jax import lax
from jax.experimental import pallas as pl
from jax.experimental.pallas import tpu as pltpu

F32 = jnp.float32
BF16 = jnp.bfloat16

HEAD_DIM = 128
MOBA_HEADS = 8
MOBA_BLOCK = 256
MOBA_TOPK = 3
MOBA_QTILE = 512
SB_HEADS = 8
POOL_WINDOWS = (2, 4, 8, 16)
POOL_GROUP = 256
POOL_HALO = 16
REL_BUCKETS = 32
REL_MAX_EXACT = 16
REL_MAX_DIST = 2048
EPS = 1e-6
NEG = -1e30
LOG2E = math.log2(math.e)
EXP2_ZERO_BELOW = -152.0

V7X_LANES = 128
V7X_VMEM_LIMIT_BYTES = 56 * 1024 * 1024

_NT = (((1,), (1,)), ((), ()))


def _params(*semantics):
    return pltpu.CompilerParams(dimension_semantics=semantics,
                                vmem_limit_bytes=V7X_VMEM_LIMIT_BYTES)


def _pick_tile(n, want, align):
    t = min(want, n)
    while t > align and (n % t or t % align):
        t -= align
    assert n % t == 0 and t % align == 0, (n, want, align)
    return t


def _rms_normalize(x, g):
    ms = jnp.mean(x * x, axis=-1, keepdims=True)
    return x * lax.rsqrt(ms + EPS) * g


NORM_CHUNK = 256


def _norm_matmul_kernel(x_ref, g_ref, w_ref, o_ref, xn_ref):
    @pl.when(pl.program_id(1) == 0)
    def _():
        w = w_ref[...].astype(BF16)
        for r0 in range(0, x_ref.shape[0], NORM_CHUNK):
            rows = pl.ds(r0, NORM_CHUNK)
            xn = _rms_normalize(x_ref[rows, :], g_ref[...]).astype(BF16)
            xn_ref[rows, :] = xn
            o_ref[rows, :] = jnp.dot(xn, w, preferred_element_type=F32).astype(o_ref.dtype)

    @pl.when(pl.program_id(1) > 0)
    def _():
        o_ref[...] = jnp.dot(xn_ref[...], w_ref[...].astype(BF16),
                             preferred_element_type=F32).astype(o_ref.dtype)


def norm_matmul(x, g, w, layer, *, out_dtype=BF16, tm=1024, tn=1024):
    T, K = x.shape
    N = w.shape[2]
    tm = _pick_tile(T, tm, 16)
    tn = _pick_tile(N, tn, V7X_LANES)
    return pl.pallas_call(
        _norm_matmul_kernel,
        out_shape=jax.ShapeDtypeStruct((T, N), out_dtype),
        grid=(T // tm, N // tn),
        in_specs=[pl.BlockSpec((tm, K), lambda i, j: (i, 0)),
                  pl.BlockSpec((1, K), lambda i, j: (0, 0)),
                  pl.BlockSpec((None, K, tn), lambda i, j: (layer, 0, j))],
        out_specs=pl.BlockSpec((tm, tn), lambda i, j: (i, j)),
        scratch_shapes=[pltpu.VMEM((tm, K), BF16)],
        compiler_params=_params("parallel", "arbitrary"),
        name="norm_matmul",
    )(x, g.reshape(1, K), w)


def _ffn_up_kernel(x_ref, g_ref, wg_ref, wu_ref, o_ref, xn_ref):
    def swiglu(xn, wg, wu):
        a = jnp.dot(xn, wg, preferred_element_type=F32)
        b = jnp.dot(xn, wu, preferred_element_type=F32)
        return (a * jax.nn.sigmoid(a) * b).astype(o_ref.dtype)

    @pl.when(pl.program_id(1) == 0)
    def _():
        wg, wu = wg_ref[...].astype(BF16), wu_ref[...].astype(BF16)
        for r0 in range(0, x_ref.shape[0], NORM_CHUNK):
            rows = pl.ds(r0, NORM_CHUNK)
            xn = _rms_normalize(x_ref[rows, :], g_ref[...]).astype(BF16)
            xn_ref[rows, :] = xn
            o_ref[rows, :] = swiglu(xn, wg, wu)

    @pl.when(pl.program_id(1) > 0)
    def _():
        o_ref[...] = swiglu(xn_ref[...], wg_ref[...].astype(BF16), wu_ref[...].astype(BF16))


def ffn_up(x, g, wg, wu, layer, *, tm=1024, tn=512):
    T, K = x.shape
    N = wg.shape[2]
    tm = _pick_tile(T, tm, 16)
    tn = _pick_tile(N, tn, V7X_LANES)
    return pl.pallas_call(
        _ffn_up_kernel,
        out_shape=jax.ShapeDtypeStruct((T, N), BF16),
        grid=(T // tm, N // tn),
        in_specs=[pl.BlockSpec((tm, K), lambda i, j: (i, 0)),
                  pl.BlockSpec((1, K), lambda i, j: (0, 0)),
                  pl.BlockSpec((None, K, tn), lambda i, j: (layer, 0, j)),
                  pl.BlockSpec((None, K, tn), lambda i, j: (layer, 0, j))],
        out_specs=pl.BlockSpec((tm, tn), lambda i, j: (i, j)),
        scratch_shapes=[pltpu.VMEM((tm, K), BF16)],
        compiler_params=_params("parallel", "arbitrary"),
        name="ffn_up",
    )(x, g.reshape(1, K), wg, wu)


def _matmul_res_kernel(x_ref, w_ref, r_ref, o_ref):
    o_ref[...] = r_ref[...] + jnp.dot(x_ref[...], w_ref[...], preferred_element_type=F32)


def matmul_res(x, w, layer, res, *, tm=1024, tn=512):
    T, K = x.shape
    N = w.shape[2]
    tm = _pick_tile(T, tm, 16)
    tn = _pick_tile(N, tn, V7X_LANES)
    return pl.pallas_call(
        _matmul_res_kernel,
        out_shape=jax.ShapeDtypeStruct((T, N), F32),
        grid=(T // tm, N // tn),
        in_specs=[pl.BlockSpec((tm, K), lambda i, j: (i, 0)),
                  pl.BlockSpec((None, K, tn), lambda i, j: (layer, 0, j)),
                  pl.BlockSpec((tm, tn), lambda i, j: (i, j))],
        out_specs=pl.BlockSpec((tm, tn), lambda i, j: (i, j)),
        compiler_params=_params("parallel", "parallel"),
        name="matmul_res",
    )(x, w, res)


def _merge_kernel(ya_ref, yb_ref, yc_ref, ga_ref, gb_ref, gc_ref, wa_ref, wb_ref, wc_ref, o_ref):
    def branch(y_ref, g_ref, w_ref):
        gate = jax.nn.sigmoid(g_ref[...].astype(F32))
        return gate * jnp.dot(y_ref[...], w_ref[...], preferred_element_type=F32)

    m = branch(ya_ref, ga_ref, wa_ref) + branch(yb_ref, gb_ref, wb_ref) + branch(yc_ref, gc_ref, wc_ref)
    o_ref[...] = m.astype(o_ref.dtype)


def merge_branches(ya, yb, yc, proj, gate_off, wa, wb, wc, layer, *, tm=1024, tn=1024):
    T, K = ya.shape
    N = wa.shape[2]
    tm = _pick_tile(T, tm, 16)
    tn = _pick_tile(N, tn, V7X_LANES)
    assert gate_off % tn == 0
    y_spec = pl.BlockSpec((tm, K), lambda i, j: (i, 0))
    w_spec = pl.BlockSpec((None, K, tn), lambda i, j: (layer, 0, j))

    def gate_spec(branch):
        base = (gate_off + branch * N) // tn
        return pl.BlockSpec((tm, tn), lambda i, j: (i, base + j))

    return pl.pallas_call(
        _merge_kernel,
        out_shape=jax.ShapeDtypeStruct((T, N), BF16),
        grid=(T // tm, N // tn),
        in_specs=[y_spec, y_spec, y_spec, gate_spec(0), gate_spec(1), gate_spec(2),
                  w_spec, w_spec, w_spec],
        out_specs=pl.BlockSpec((tm, tn), lambda i, j: (i, j)),
        compiler_params=_params("parallel", "parallel"),
        name="merge_branches",
    )(ya, yb, yc, proj, proj, proj, wa, wb, wc)


def _final_norm_kernel(x_ref, g_ref, o_ref):
    o_ref[...] = _rms_normalize(x_ref[...], g_ref[...])


def final_norm(x, g, *, tm=512):
    T, K = x.shape
    tm = _pick_tile(T, tm, 8)
    return pl.pallas_call(
        _final_norm_kernel,
        out_shape=jax.ShapeDtypeStruct((T, K), F32),
        grid=(T // tm,),
        in_specs=[pl.BlockSpec((tm, K), lambda i: (i, 0)),
                  pl.BlockSpec((1, K), lambda i: (0, 0))],
        out_specs=pl.BlockSpec((tm, K), lambda i: (i, 0)),
        compiler_params=_params("parallel"),
        name="final_norm",
    )(x, g.reshape(1, K))


POOL_PAD = 8


def _pool_kernel(u_ref, halo_ref, w_ref, sc_ref, o_ref, ext_ref, stage_ref, *, tm, tiles_per_seq):
    t_in_seq = pl.program_id(0) % tiles_per_seq
    base = POOL_PAD + POOL_HALO
    rows = tm + POOL_HALO
    zeros = jnp.zeros((POOL_PAD, ext_ref.shape[1]), F32)
    ext_ref[0:POOL_PAD, :] = zeros
    ext_ref[POOL_PAD:base, :] = jnp.where(t_in_seq == 0, 0.0, halo_ref[...].astype(F32))
    ext_ref[base:, :] = u_ref[...].astype(F32)
    stage_ref[0:POOL_PAD, :] = zeros[:, 0:POOL_GROUP]
    pos = t_in_seq * tm + lax.broadcasted_iota(jnp.int32, (tm, 1), 0)
    for gi, win in enumerate(POOL_WINDOWS):
        cols = slice(gi * POOL_GROUP, (gi + 1) * POOL_GROUP)
        x = ext_ref[base:, cols]
        cur = ext_ref[pl.ds(POOL_PAD, rows), cols] + ext_ref[pl.ds(POOL_PAD - 1, rows), cols]
        span = 2
        while span < win:
            stage_ref[pl.ds(POOL_PAD, rows), :] = cur
            cur = cur + stage_ref[pl.ds(POOL_PAD - span, rows), :]
            span *= 2
        assert span == win <= 2 * POOL_PAD, "pooling windows must be powers of two up to 2 * POOL_PAD"
        total = cur[POOL_HALO:]
        cnt = jnp.minimum(pos + 1, win).astype(F32)
        pooled = total / cnt - x
        y = jnp.dot(pooled.astype(BF16), w_ref[gi], preferred_element_type=F32)
        o_ref[:, cols] = (y * sc_ref[:, cols]).astype(o_ref.dtype)


def pool_mixer(proj, u_off, w_grp, layer, scale, seq_len, *, tm=1024):
    T = proj.shape[0]
    width = len(POOL_WINDOWS) * POOL_GROUP
    tm = _pick_tile(seq_len, tm, POOL_HALO)
    assert u_off % width == 0 and max(POOL_WINDOWS) <= POOL_HALO
    ublk = u_off // width
    rows_per_tile = tm // POOL_HALO
    return pl.pallas_call(
        functools.partial(_pool_kernel, tm=tm, tiles_per_seq=seq_len // tm),
        out_shape=jax.ShapeDtypeStruct((T, width), BF16),
        grid=(T // tm,),
        in_specs=[pl.BlockSpec((tm, width), lambda i: (i, ublk)),
                  pl.BlockSpec((POOL_HALO, width),
                               lambda i: (jnp.maximum(i * rows_per_tile - 1, 0), ublk)),
                  pl.BlockSpec((None,) + w_grp.shape[1:], lambda i: (layer, 0, 0, 0)),
                  pl.BlockSpec((1, width), lambda i: (0, 0))],
        out_specs=pl.BlockSpec((tm, width), lambda i: (i, 0)),
        scratch_shapes=[pltpu.VMEM((POOL_PAD + POOL_HALO + tm, width), F32),
                        pltpu.VMEM((POOL_PAD + POOL_HALO + tm, POOL_GROUP), F32)],
        compiler_params=_params("parallel"),
        name="pool_mixer",
    )(proj, proj, w_grp, scale.reshape(1, width))


CAST_COLS = 1024


def _cast_plan(casts, grid):
    nsteps = grid[0] * grid[1] * grid[2]
    step = lambda h, b, i: (h * grid[1] + b) * grid[2] + i
    operands, in_specs, out_shapes, out_specs, shapes = [], [], [], [], []
    for w, layer in casts:
        depth, K, N = w.shape
        total_rows = K * N // CAST_COLS
        rows = total_rows // nsteps
        assert K * N % CAST_COLS == 0 and total_rows % nsteps == 0 and rows % 16 == 0, (w.shape, nsteps)
        operands.append(w.reshape(depth, total_rows, CAST_COLS))
        in_specs.append(pl.BlockSpec((None, rows, CAST_COLS),
                                     lambda h, b, i, layer=layer: (layer, step(h, b, i), 0)))
        out_shapes.append(jax.ShapeDtypeStruct((total_rows, CAST_COLS), BF16))
        out_specs.append(pl.BlockSpec((rows, CAST_COLS), lambda h, b, i: (step(h, b, i), 0)))
        shapes.append((1, K, N))
    return operands, in_specs, out_shapes, out_specs, shapes


def _cast_slabs(src_refs, dst_refs):
    for src, dst in zip(src_refs, dst_refs):
        dst[...] = src[...].astype(BF16)


def _bucket_range(d_lo, d_hi):
    def bucket(d):
        if d < REL_MAX_EXACT:
            return d
        big = REL_MAX_EXACT + int(math.log(d / REL_MAX_EXACT) / math.log(REL_MAX_DIST / REL_MAX_EXACT)
                                  * (REL_BUCKETS - REL_MAX_EXACT))
        return min(big, REL_BUCKETS - 1)
    return max(bucket(max(d_lo, 0)) - 1, 0), min(bucket(d_hi) + 1, REL_BUCKETS - 1)


BIAS_ROWS = 8


def _bias_tiles_kernel(tab_ref, o_ref, *, heads, blk, ndist):
    key0 = lax.broadcasted_iota(jnp.int32, (BIAS_ROWS, blk), 0)
    qry = lax.broadcasted_iota(jnp.int32, (BIAS_ROWS, blk), 1)
    for dist in range(ndist):
        @pl.when(pl.program_id(0) == dist)
        def _(dist=dist):
            b_lo, b_hi = _bucket_range(dist * blk - (blk - 1), dist * blk + (blk - 1))
            table = [[tab_ref[h * REL_BUCKETS + b] * LOG2E for b in range(b_lo, b_hi + 1)]
                     for h in range(heads)]

            def rows(c, carry):
                r0 = pl.multiple_of(c * BIAS_ROWS, BIAS_ROWS)
                d = dist * blk + qry - (key0 + r0)
                n = jnp.maximum(d, 0)
                nf = jnp.maximum(n, 1).astype(F32)
                large = REL_MAX_EXACT + (jnp.log(nf / REL_MAX_EXACT)
                                         / math.log(REL_MAX_DIST / REL_MAX_EXACT)
                                         * (REL_BUCKETS - REL_MAX_EXACT)).astype(jnp.int32)
                large = jnp.minimum(large, REL_BUCKETS - 1)
                bucket = jnp.where(n < REL_MAX_EXACT, n, large)
                accs = [jnp.full((BIAS_ROWS, blk), table[h][-1], F32) for h in range(heads)]
                for i, b in enumerate(range(b_lo, b_hi)):
                    mask = bucket == b
                    accs = [jnp.where(mask, table[h][i], accs[h]) for h in range(heads)]
                for h in range(heads):
                    val = jnp.where(d >= 0, accs[h], NEG) if dist == 0 else accs[h]
                    o_ref[h, 0, pl.ds(r0, BIAS_ROWS), :] = val
                return carry

            lax.fori_loop(0, blk // BIAS_ROWS, rows, 0)


def rel_bias_tiles(rel_table, ndist, blk):
    heads = rel_table.shape[0]
    return pl.pallas_call(
        functools.partial(_bias_tiles_kernel, heads=heads, blk=blk, ndist=ndist),
        out_shape=jax.ShapeDtypeStruct((heads, ndist, blk, blk), F32),
        grid=(ndist,),
        in_specs=[pl.BlockSpec(memory_space=pltpu.SMEM)],
        out_specs=pl.BlockSpec((heads, 1, blk, blk), lambda d: (0, d, 0, 0)),
        compiler_params=_params("parallel"),
        name="rel_bias_tiles",
    )(rel_table.reshape(-1))


MOBA_VT_ROWS = HEAD_DIM + 8


def _moba_kernel(*refs, ncast, nb, blk, topk, scale):
    q_ref, k_ref, v_ref, bias_ref = refs[:4]
    o_ref = refs[4 + ncast]
    kaug_ref, vt_ref, kbar_ref, s_ref = refs[5 + 2 * ncast:]
    _cast_slabs(refs[4:4 + ncast], refs[5 + ncast:5 + 2 * ncast])
    qi = pl.program_id(2)
    nstream, tq, dh = q_ref.shape
    seq = k_ref.shape[1]
    nbp = kbar_ref.shape[1]

    @pl.when(qi == 0)
    def _():
        row_blk = lax.broadcasted_iota(jnp.int32, (seq, dh), 0) // blk
        col = lax.broadcasted_iota(jnp.int32, (seq, dh), 1)
        onehot = (row_blk == col).astype(BF16)
        ones_rows = (lax.broadcasted_iota(jnp.int32, (MOBA_VT_ROWS - dh, tq), 0) == 0).astype(BF16)
        for r in range(nstream):
            kaug_ref[r, :, 0:dh] = (k_ref[r].astype(F32) * (scale * LOG2E)).astype(BF16)
            kaug_ref[r, :, dh:] = onehot
            for t in range(seq // tq):
                vt_ref[r, t, 0:dh, :] = v_ref[r, t * tq:(t + 1) * tq, :].astype(F32).T.astype(BF16)
                vt_ref[r, t, dh:, :] = ones_rows
            kbar_ref[r] = jnp.zeros(kbar_ref.shape[1:], F32)
            for n in range(nb):
                kblk = k_ref[r, n * blk:(n + 1) * blk, :].astype(F32)
                kbar_ref[r, n:n + 1, :] = jnp.mean(kblk, axis=0, keepdims=True)

    blk_id = lax.broadcasted_iota(jnp.int32, (nbp, tq), 0)
    own = qi * (tq // blk) + lax.broadcasted_iota(jnp.int32, (nbp, tq), 1) // blk
    past = blk_id < own

    def augmented_query(r):
        q = q_ref[r]
        kbar = kbar_ref[r]
        kbar_hi = kbar.astype(BF16)
        kbar_lo = (kbar - kbar_hi.astype(F32)).astype(BF16)
        gate = (lax.dot_general(kbar_hi, q, _NT, preferred_element_type=F32)
                + lax.dot_general(kbar_lo, q, _NT, preferred_element_type=F32))
        gate = jnp.where(past, gate, -jnp.inf)
        rank = jnp.zeros((nbp, tq), jnp.int32)
        for m in range(nb - 1):
            gm = gate[m:m + 1, :]
            before = (gm > gate) | ((gm == gate) & (m < blk_id))
            rank = rank + before.astype(jnp.int32)
        chosen = (past & (rank < topk)) | (blk_id == own)
        pen_t = jnp.where(chosen, 0.0, NEG)
        pen_t = jnp.concatenate([pen_t, jnp.zeros((dh - nbp, tq), F32)], axis=0)
        return jnp.concatenate([q, pen_t.T.astype(BF16)], axis=1)

    q_aug = [augmented_query(r) for r in range(nstream)]

    def produce(r, dist):
        start = pl.multiple_of((qi - dist) * tq, tq)
        s = lax.dot_general(kaug_ref[r, pl.ds(start, tq), :], q_aug[r], _NT,
                            preferred_element_type=F32) + bias_ref[dist]
        s_ref[dist % 2, r] = s
        return jnp.max(s, axis=0, keepdims=True)

    def absorb(r, dist, s_max, m_i, acc):
        m_new = jnp.maximum(m_i, s_max)
        p = jnp.exp2(s_ref[dist % 2, r] - m_new)
        acc = (jnp.exp2(m_i - m_new) * acc
               + jnp.dot(vt_ref[r, qi - dist], p.astype(BF16), preferred_element_type=F32))
        return m_new, acc

    def body(dist, carry):
        done = [absorb(r, dist, *carry[r]) for r in range(nstream)]
        return tuple((produce(r, dist + 1),) + done[r] for r in range(nstream))

    init = tuple((produce(r, 0), jnp.full((1, tq), NEG, F32), jnp.zeros((MOBA_VT_ROWS, tq), F32))
                 for r in range(nstream))
    last = lax.fori_loop(0, qi, body, init)
    for r in range(nstream):
        _, acc = absorb(r, qi, *last[r])
        o_ref[r] = (acc[0:dh] / acc[dh:dh + 1]).T.astype(o_ref.dtype)


def moba_attention(proj3, q_off, k_off, v_off, bias_tiles, heads, *, nstream=4, casts=()):
    B, S, _ = proj3.shape
    blk, dh = MOBA_BLOCK, HEAD_DIM
    tq = bias_tiles.shape[-1]
    assert S % tq == 0 and tq % blk == 0 and q_off % dh == 0 and k_off % dh == 0 and v_off % dh == 0
    nstream = nstream if B % nstream == 0 else 1
    nb = S // blk
    nbp = -(-nb // 8) * 8
    assert nbp <= dh, "block one-hot columns must fit beside the keys"
    topk = min(MOBA_TOPK, nb)
    qb, kb_, vb_ = q_off // dh, k_off // dh, v_off // dh
    grid = (heads, B // nstream, S // tq)
    c_ops, c_in, c_shapes, c_out, c_views = _cast_plan(casts, grid)
    kern = functools.partial(_moba_kernel, ncast=len(casts), nb=nb, blk=blk, topk=topk, scale=dh ** -0.5)
    out, *cast = pl.pallas_call(
        kern,
        out_shape=[jax.ShapeDtypeStruct((B, S, heads * dh), BF16)] + c_shapes,
        grid=grid,
        in_specs=[pl.BlockSpec((nstream, tq, dh), lambda h, b, i: (b, i, qb + h)),
                  pl.BlockSpec((nstream, S, dh), lambda h, b, i: (b, 0, kb_ + h),
                               pipeline_mode=pl.Buffered(1)),
                  pl.BlockSpec((nstream, S, dh), lambda h, b, i: (b, 0, vb_ + h),
                               pipeline_mode=pl.Buffered(1)),
                  pl.BlockSpec((None, S // tq, tq, tq), lambda h, b, i: (h, 0, 0, 0),
                               pipeline_mode=pl.Buffered(1))] + c_in,
        out_specs=[pl.BlockSpec((nstream, tq, dh), lambda h, b, i: (b, i, h))] + c_out,
        scratch_shapes=[pltpu.VMEM((nstream, S, 2 * dh), BF16),
                        pltpu.VMEM((nstream, S // tq, MOBA_VT_ROWS, tq), BF16),
                        pltpu.VMEM((nstream, nbp, dh), F32),
                        pltpu.VMEM((2, nstream, tq, tq), F32)],
        compiler_params=_params("parallel", "parallel", "arbitrary"),
        name="moba_attention",
    )(proj3, proj3, proj3, bias_tiles, *c_ops)
    return out, [c.reshape(v) for c, v in zip(cast, c_views)]


def _suffix_sum_matrix(bk):
    r = lax.broadcasted_iota(jnp.int32, (2 * bk, 2 * bk), 0) % bk
    c = lax.broadcasted_iota(jnp.int32, (2 * bk, 2 * bk), 1)
    return ((c >= bk) | (r > c)).astype(BF16)


def _sb_kernel(*refs, ncast, scale):
    q_ref, k_ref, v_ref = refs[:3]
    o_ref = refs[3 + ncast]
    _cast_slabs(refs[3:3 + ncast], refs[4 + ncast:4 + 2 * ncast])
    qi = pl.program_id(2)
    nbatch, tq, width = q_ref.shape
    dh = HEAD_DIM
    half = tq // 2
    sums = _suffix_sum_matrix(half)
    streams = [(r, c0) for r in range(nbatch) for c0 in range(0, width, dh)]
    nstream = len(streams)
    q2 = [(q_ref[r, :, c0:c0 + dh].astype(F32) * (scale * LOG2E)).astype(BF16) for r, c0 in streams]

    def tile(s, j, run, acc, valid):
        r, c0 = streams[s]
        start = pl.multiple_of(j * tq, tq)
        z = lax.dot_general(q2[s], k_ref[r, pl.ds(start, tq), c0:c0 + dh], _NT,
                            preferred_element_type=F32)
        sp = jnp.maximum(z, 0.0) + jnp.log2(1.0 + jnp.exp2(-jnp.abs(z)))
        log_1m = -sp
        if valid is not None:
            log_1m = jnp.where(valid, log_1m, 0.0)
        hi = log_1m.astype(BF16)
        lo = (log_1m - hi.astype(F32)).astype(BF16)
        cs_b = jnp.dot(jnp.concatenate([hi[:, half:], lo[:, half:]], axis=1), sums,
                       preferred_element_type=F32)
        cs_a = jnp.dot(jnp.concatenate([hi[:, :half], lo[:, :half]], axis=1), sums,
                       preferred_element_type=F32)
        run_b = run + cs_b[:, half:]
        after = jnp.concatenate([run_b + cs_a[:, :half], run + cs_b[:, :half]], axis=1)
        a = jnp.exp2((z - sp) + after)
        if valid is not None:
            a = jnp.where(valid, a, 0.0)
        acc = acc + jnp.dot(a.astype(BF16), v_ref[r, pl.ds(start, tq), c0:c0 + dh],
                            preferred_element_type=F32)
        return run_b + cs_a[:, half:], acc

    row = lax.broadcasted_iota(jnp.int32, (tq, tq), 0)
    col = lax.broadcasted_iota(jnp.int32, (tq, tq), 1)
    no_prev = jnp.where(qi > 0, 0.0, 4.0 * EXP2_ZERO_BELOW)
    state = []
    for s in range(nstream):
        run, acc = tile(s, qi, jnp.zeros((tq, half), F32), jnp.zeros((tq, dh), F32), col < row)
        state.append(tile(s, jnp.maximum(qi - 1, 0), run + no_prev, acc, None))

    def run_max(state):
        return functools.reduce(jnp.maximum, [jnp.max(run) for run, _ in state])

    def cond(carry):
        j, _, top = carry
        return (j >= 0) & (top > EXP2_ZERO_BELOW)

    def body(carry):
        j, state, _ = carry
        state = tuple(tile(s, j, run, acc, None) for s, (run, acc) in enumerate(state))
        return j - 1, state, run_max(state)

    _, state, _ = lax.while_loop(cond, body, (qi - 2, tuple(state), run_max(state)))
    for (r, c0), (_, acc) in zip(streams, state):
        o_ref[r, :, c0:c0 + dh] = acc.astype(o_ref.dtype)


def stick_breaking_attention(proj3, q_off, k_off, v_off, heads, *, tq=2 * V7X_LANES, nbatch=4, nhead=2,
                             casts=()):
    B, S, _ = proj3.shape
    dh = HEAD_DIM
    assert tq == 2 * V7X_LANES, "suffix sums work on 128-key halves of a tile"
    nbatch = nbatch if B % nbatch == 0 else 1
    width = nhead * dh
    assert S % tq == 0 and heads % nhead == 0
    assert q_off % width == 0 and k_off % width == 0 and v_off % width == 0
    qb, kb_, vb_ = q_off // width, k_off // width, v_off // width
    grid = (heads // nhead, B // nbatch, S // tq)
    c_ops, c_in, c_shapes, c_out, c_views = _cast_plan(casts, grid)
    out, *cast = pl.pallas_call(
        functools.partial(_sb_kernel, ncast=len(casts), scale=dh ** -0.5),
        out_shape=[jax.ShapeDtypeStruct((B, S, heads * dh), BF16)] + c_shapes,
        grid=grid,
        in_specs=[pl.BlockSpec((nbatch, tq, width), lambda h, b, i: (b, i, qb + h)),
                  pl.BlockSpec((nbatch, S, width), lambda h, b, i: (b, 0, kb_ + h)),
                  pl.BlockSpec((nbatch, S, width), lambda h, b, i: (b, 0, vb_ + h))] + c_in,
        out_specs=[pl.BlockSpec((nbatch, tq, width), lambda h, b, i: (b, i, h))] + c_out,
        compiler_params=_params("parallel", "parallel", "parallel"),
        name="stick_breaking_attention",
    )(proj3, proj3, proj3, *c_ops)
    return out, [c.reshape(v) for c, v in zip(cast, c_views)]


def kernel(x, norm_mix, norm_ffn, w_in, w_pool, pool_scale, w_br_a, w_br_b, w_br_c, w_out, w_gate,
           w_up, w_down, rel_bias, norm_final):
    B, S, D = x.shape
    depth = w_in.shape[0]
    T = B * S
    moba_w = MOBA_HEADS * HEAD_DIM
    sb_w = SB_HEADS * HEAD_DIM
    pool_w = len(POOL_WINDOWS) * POOL_GROUP
    off_qa, off_ka, off_va = 0, moba_w, 2 * moba_w
    off_u = 3 * moba_w
    off_qs = off_u + pool_w
    off_ks, off_vs = off_qs + sb_w, off_qs + 2 * sb_w
    off_gates = off_qs + 3 * sb_w

    bias_tiles = rel_bias_tiles(rel_bias, S // MOBA_QTILE, MOBA_QTILE)
    w_pool, w_br_a, w_br_b, w_br_c, w_out = (
        w.astype(BF16) for w in (w_pool, w_br_a, w_br_b, w_br_c, w_out))
    h = x.reshape(T, D)
    w_in_l, layer_in = w_in, 0
    for l in range(depth):
        proj = norm_matmul(h, norm_mix[l], w_in_l, layer_in)
        proj3 = proj.reshape(B, S, proj.shape[1])
        moba_casts = [(w_down, l)] + ([(w_in, l + 1)] if l + 1 < depth else [])
        ya, moba_cast = moba_attention(proj3, off_qa, off_ka, off_va, bias_tiles, MOBA_HEADS,
                                       casts=moba_casts)
        yb = pool_mixer(proj, off_u, w_pool, l, pool_scale[l], S)
        yc, (w_gate_l, w_up_l) = stick_breaking_attention(proj3, off_qs, off_ks, off_vs, SB_HEADS,
                                                          casts=[(w_gate, l), (w_up, l)])
        m = merge_branches(ya.reshape(T, moba_w), yb, yc.reshape(T, sb_w), proj, off_gates,
                           w_br_a, w_br_b, w_br_c, l)
        h = matmul_res(m, w_out, l, h, tn=1024)
        act = ffn_up(h, norm_ffn[l], w_gate_l, w_up_l, 0)
        h = matmul_res(act, moba_cast[0], 0, h)
        if l + 1 < depth:
            w_in_l, layer_in = moba_cast[1], 0
    return final_norm(h, norm_final).reshape(B, S, D)
```

```python
import functools
import math

import jax
import jax.numpy as jnp
from jax import lax
from jax.experimental import pallas as pl
from jax.experimental.pallas import tpu as pltpu

F32 = jnp.float32
BF16 = jnp.bfloat16

HEAD_DIM = 128
MOBA_HEADS = 8
MOBA_BLOCK = 256
MOBA_TOPK = 3
MOBA_QTILE = 512
SB_HEADS = 8
POOL_WINDOWS = (2, 4, 8, 16)
POOL_GROUP = 256
POOL_HALO = 16
REL_BUCKETS = 32
REL_MAX_EXACT = 16
REL_MAX_DIST = 2048
EPS = 1e-6
NEG = -1e30
LOG2E = math.log2(math.e)
EXP2_ZERO_BELOW = -152.0

V7X_LANES = 128
V7X_VMEM_LIMIT_BYTES = 56 * 1024 * 1024

_NT = (((1,), (1,)), ((), ()))


def _params(*semantics):
    return pltpu.CompilerParams(dimension_semantics=semantics,
                                vmem_limit_bytes=V7X_VMEM_LIMIT_BYTES)


def _pick_tile(n, want, align):
    t = min(want, n)
    while t > align and (n % t or t % align):
        t -= align
    assert n % t == 0 and t % align == 0, (n, want, align)
    return t


def _rms_normalize(x, g):
    ms = jnp.mean(x * x, axis=-1, keepdims=True)
    return x * lax.rsqrt(ms + EPS) * g


NORM_CHUNK = 256


def _norm_matmul_kernel(x_ref, g_ref, w_ref, o_ref, xn_ref):
    @pl.when(pl.program_id(1) == 0)
    def _():
        w = w_ref[...].astype(BF16)
        for r0 in range(0, x_ref.shape[0], NORM_CHUNK):
            rows = pl.ds(r0, NORM_CHUNK)
            xn = _rms_normalize(x_ref[rows, :], g_ref[...]).astype(BF16)
            xn_ref[rows, :] = xn
            o_ref[rows, :] = jnp.dot(xn, w, preferred_element_type=F32).astype(o_ref.dtype)

    @pl.when(pl.program_id(1) > 0)
    def _():
        o_ref[...] = jnp.dot(xn_ref[...], w_ref[...].astype(BF16),
                             preferred_element_type=F32).astype(o_ref.dtype)


def norm_matmul(x, g, w, layer, *, out_dtype=BF16, tm=1024, tn=1024):
    T, K = x.shape
    N = w.shape[2]
    tm = _pick_tile(T, tm, 16)
    tn = _pick_tile(N, tn, V7X_LANES)
    return pl.pallas_call(
        _norm_matmul_kernel,
        out_shape=jax.ShapeDtypeStruct((T, N), out_dtype),
        grid=(T // tm, N // tn),
        in_specs=[pl.BlockSpec((tm, K), lambda i, j: (i, 0)),
                  pl.BlockSpec((1, K), lambda i, j: (0, 0)),
                  pl.BlockSpec((None, K, tn), lambda i, j: (layer, 0, j))],
        out_specs=pl.BlockSpec((tm, tn), lambda i, j: (i, j)),
        scratch_shapes=[pltpu.VMEM((tm, K), BF16)],
        compiler_params=_params("parallel", "arbitrary"),
        name="norm_matmul",
    )(x, g.reshape(1, K), w)


def _ffn_up_kernel(x_ref, g_ref, wg_ref, wu_ref, o_ref, xn_ref):
    def swiglu(xn, wg, wu):
        a = jnp.dot(xn, wg, preferred_element_type=F32)
        b = jnp.dot(xn, wu, preferred_element_type=F32)
        return (a * jax.nn.sigmoid(a) * b).astype(o_ref.dtype)

    @pl.when(pl.program_id(1) == 0)
    def _():
        wg, wu = wg_ref[...].astype(BF16), wu_ref[...].astype(BF16)
        for r0 in range(0, x_ref.shape[0], NORM_CHUNK):
            rows = pl.ds(r0, NORM_CHUNK)
            xn = _rms_normalize(x_ref[rows, :], g_ref[...]).astype(BF16)
            xn_ref[rows, :] = xn
            o_ref[rows, :] = swiglu(xn, wg, wu)

    @pl.when(pl.program_id(1) > 0)
    def _():
        o_ref[...] = swiglu(xn_ref[...], wg_ref[...].astype(BF16), wu_ref[...].astype(BF16))


def ffn_up(x, g, wg, wu, layer, *, tm=1024, tn=512):
    T, K = x.shape
    N = wg.shape[2]
    tm = _pick_tile(T, tm, 16)
    tn = _pick_tile(N, tn, V7X_LANES)
    return pl.pallas_call(
        _ffn_up_kernel,
        out_shape=jax.ShapeDtypeStruct((T, N), BF16),
        grid=(T // tm, N // tn),
        in_specs=[pl.BlockSpec((tm, K), lambda i, j: (i, 0)),
                  pl.BlockSpec((1, K), lambda i, j: (0, 0)),
                  pl.BlockSpec((None, K, tn), lambda i, j: (layer, 0, j)),
                  pl.BlockSpec((None, K, tn), lambda i, j: (layer, 0, j))],
        out_specs=pl.BlockSpec((tm, tn), lambda i, j: (i, j)),
        scratch_shapes=[pltpu.VMEM((tm, K), BF16)],
        compiler_params=_params("parallel", "arbitrary"),
        name="ffn_up",
    )(x, g.reshape(1, K), wg, wu)


def _matmul_res_kernel(x_ref, w_ref, r_ref, o_ref):
    o_ref[...] = r_ref[...] + jnp.dot(x_ref[...], w_ref[...], preferred_element_type=F32)


def matmul_res(x, w, layer, res, *, tm=1024, tn=512):
    T, K = x.shape
    N = w.shape[2]
    tm = _pick_tile(T, tm, 16)
    tn = _pick_tile(N, tn, V7X_LANES)
    return pl.pallas_call(
        _matmul_res_kernel,
        out_shape=jax.ShapeDtypeStruct((T, N), F32),
        grid=(T // tm, N // tn),
        in_specs=[pl.BlockSpec((tm, K), lambda i, j: (i, 0)),
                  pl.BlockSpec((None, K, tn), lambda i, j: (layer, 0, j)),
                  pl.BlockSpec((tm, tn), lambda i, j: (i, j))],
        out_specs=pl.BlockSpec((tm, tn), lambda i, j: (i, j)),
        compiler_params=_params("parallel", "parallel"),
        name="matmul_res",
    )(x, w, res)


def _merge_kernel(ya_ref, yb_ref, yc_ref, ga_ref, gb_ref, gc_ref, wa_ref, wb_ref, wc_ref, o_ref):
    def branch(y_ref, g_ref, w_ref):
        gate = jax.nn.sigmoid(g_ref[...].astype(F32))
        return gate * jnp.dot(y_ref[...], w_ref[...], preferred_element_type=F32)

    m = branch(ya_ref, ga_ref, wa_ref) + branch(yb_ref, gb_ref, wb_ref) + branch(yc_ref, gc_ref, wc_ref)
    o_ref[...] = m.astype(o_ref.dtype)


def merge_branches(ya, yb, yc, proj, gate_off, wa, wb, wc, layer, *, tm=1024, tn=1024):
    T, K = ya.shape
    N = wa.shape[2]
    tm = _pick_tile(T, tm, 16)
    tn = _pick_tile(N, tn, V7X_LANES)
    assert gate_off % tn == 0
    y_spec = pl.BlockSpec((tm, K), lambda i, j: (i, 0))
    w_spec = pl.BlockSpec((None, K, tn), lambda i, j: (layer, 0, j))

    def gate_spec(branch):
        base = (gate_off + branch * N) // tn
        return pl.BlockSpec((tm, tn), lambda i, j: (i, base + j))

    return pl.pallas_call(
        _merge_kernel,
        out_shape=jax.ShapeDtypeStruct((T, N), BF16),
        grid=(T // tm, N // tn),
        in_specs=[y_spec, y_spec, y_spec, gate_spec(0), gate_spec(1), gate_spec(2),
                  w_spec, w_spec, w_spec],
        out_specs=pl.BlockSpec((tm, tn), lambda i, j: (i, j)),
        compiler_params=_params("parallel", "parallel"),
        name="merge_branches",
    )(ya, yb, yc, proj, proj, proj, wa, wb, wc)


def _final_norm_kernel(x_ref, g_ref, o_ref):
    o_ref[...] = _rms_normalize(x_ref[...], g_ref[...])


def final_norm(x, g, *, tm=512):
    T, K = x.shape
    tm = _pick_tile(T, tm, 8)
    return pl.pallas_call(
        _final_norm_kernel,
        out_shape=jax.ShapeDtypeStruct((T, K), F32),
        grid=(T // tm,),
        in_specs=[pl.BlockSpec((tm, K), lambda i: (i, 0)),
                  pl.BlockSpec((1, K), lambda i: (0, 0))],
        out_specs=pl.BlockSpec((tm, K), lambda i: (i, 0)),
        compiler_params=_params("parallel"),
        name="final_norm",
    )(x, g.reshape(1, K))


POOL_PAD = 8


def _pool_kernel(u_ref, halo_ref, w_ref, sc_ref, o_ref, ext_ref, stage_ref, *, tm, tiles_per_seq):
    t_in_seq = pl.program_id(0) % tiles_per_seq
    base = POOL_PAD + POOL_HALO
    rows = tm + POOL_HALO
    zeros = jnp.zeros((POOL_PAD, ext_ref.shape[1]), F32)
    ext_ref[0:POOL_PAD, :] = zeros
    ext_ref[POOL_PAD:base, :] = jnp.where(t_in_seq == 0, 0.0, halo_ref[...].astype(F32))
    ext_ref[base:, :] = u_ref[...].astype(F32)
    stage_ref[0:POOL_PAD, :] = zeros[:, 0:POOL_GROUP]
    pos = t_in_seq * tm + lax.broadcasted_iota(jnp.int32, (tm, 1), 0)
    for gi, win in enumerate(POOL_WINDOWS):
        cols = slice(gi * POOL_GROUP, (gi + 1) * POOL_GROUP)
        x = ext_ref[base:, cols]
        cur = ext_ref[pl.ds(POOL_PAD, rows), cols] + ext_ref[pl.ds(POOL_PAD - 1, rows), cols]
        span = 2
        while span < win:
            stage_ref[pl.ds(POOL_PAD, rows), :] = cur
            cur = cur + stage_ref[pl.ds(POOL_PAD - span, rows), :]
            span *= 2
        assert span == win <= 2 * POOL_PAD, "pooling windows must be powers of two up to 2 * POOL_PAD"
        total = cur[POOL_HALO:]
        cnt = jnp.minimum(pos + 1, win).astype(F32)
        pooled = total / cnt - x
        y = jnp.dot(pooled.astype(BF16), w_ref[gi], preferred_element_type=F32)
        o_ref[:, cols] = (y * sc_ref[:, cols]).astype(o_ref.dtype)


def pool_mixer(proj, u_off, w_grp, layer, scale, seq_len, *, tm=1024):
    T = proj.shape[0]
    width = len(POOL_WINDOWS) * POOL_GROUP
    tm = _pick_tile(seq_len, tm, POOL_HALO)
    assert u_off % width == 0 and max(POOL_WINDOWS) <= POOL_HALO
    ublk = u_off // width
    rows_per_tile = tm // POOL_HALO
    return pl.pallas_call(
        functools.partial(_pool_kernel, tm=tm, tiles_per_seq=seq_len // tm),
        out_shape=jax.ShapeDtypeStruct((T, width), BF16),
        grid=(T // tm,),
        in_specs=[pl.BlockSpec((tm, width), lambda i: (i, ublk)),
                  pl.BlockSpec((POOL_HALO, width),
                               lambda i: (jnp.maximum(i * rows_per_tile - 1, 0), ublk)),
                  pl.BlockSpec((None,) + w_grp.shape[1:], lambda i: (layer, 0, 0, 0)),
                  pl.BlockSpec((1, width), lambda i: (0, 0))],
        out_specs=pl.BlockSpec((tm, width), lambda i: (i, 0)),
        scratch_shapes=[pltpu.VMEM((POOL_PAD + POOL_HALO + tm, width), F32),
                        pltpu.VMEM((POOL_PAD + POOL_HALO + tm, POOL_GROUP), F32)],
        compiler_params=_params("parallel"),
        name="pool_mixer",
    )(proj, proj, w_grp, scale.reshape(1, width))


BF16_ROWS = 16


def _cast_plan(casts, grid):
    nsteps = grid[0] * grid[1] * grid[2]
    step = lambda h, b, i: (h * grid[1] + b) * grid[2] + i
    operands, in_specs, out_shapes, out_specs, shapes = [], [], [], [], []
    for w, layer in casts:
        depth, K, N = w.shape
        share = 1
        while (K * share) % (nsteps * BF16_ROWS) and share < nsteps:
            share *= 2
        assert (K * share) % (nsteps * BF16_ROWS) == 0 and nsteps % share == 0, (w.shape, nsteps)
        rows = K * share // nsteps
        operands.append(w)
        in_specs.append(pl.BlockSpec((None, rows, N), lambda h, b, i, layer=layer, share=share:
                                     (layer, step(h, b, i) // share, 0)))
        out_shapes.append(jax.ShapeDtypeStruct((K, N), BF16))
        out_specs.append(pl.BlockSpec((rows, N), lambda h, b, i, share=share: (step(h, b, i) // share, 0)))
        shapes.append((1, K, N))
    return operands, in_specs, out_shapes, out_specs, shapes


def _cast_slabs(src_refs, dst_refs):
    for src, dst in zip(src_refs, dst_refs):
        dst[...] = src[...].astype(BF16)


def _bucket_range(d_lo, d_hi):
    def bucket(d):
        if d < REL_MAX_EXACT:
            return d
        big = REL_MAX_EXACT + int(math.log(d / REL_MAX_EXACT) / math.log(REL_MAX_DIST / REL_MAX_EXACT)
                                  * (REL_BUCKETS - REL_MAX_EXACT))
        return min(big, REL_BUCKETS - 1)
    return max(bucket(max(d_lo, 0)) - 1, 0), min(bucket(d_hi) + 1, REL_BUCKETS - 1)


BIAS_ROWS = 8


def _bias_tiles_kernel(tab_ref, o_ref, *, heads, blk, ndist):
    key0 = lax.broadcasted_iota(jnp.int32, (BIAS_ROWS, blk), 0)
    qry = lax.broadcasted_iota(jnp.int32, (BIAS_ROWS, blk), 1)
    for dist in range(ndist):
        @pl.when(pl.program_id(0) == dist)
        def _(dist=dist):
            b_lo, b_hi = _bucket_range(dist * blk - (blk - 1), dist * blk + (blk - 1))
            table = [[tab_ref[h * REL_BUCKETS + b] * LOG2E for b in range(b_lo, b_hi + 1)]
                     for h in range(heads)]

            def rows(c, carry):
                r0 = pl.multiple_of(c * BIAS_ROWS, BIAS_ROWS)
                d = dist * blk + qry - (key0 + r0)
                n = jnp.maximum(d, 0)
                nf = jnp.maximum(n, 1).astype(F32)
                large = REL_MAX_EXACT + (jnp.log(nf / REL_MAX_EXACT)
                                         / math.log(REL_MAX_DIST / REL_MAX_EXACT)
                                         * (REL_BUCKETS - REL_MAX_EXACT)).astype(jnp.int32)
                large = jnp.minimum(large, REL_BUCKETS - 1)
                bucket = jnp.where(n < REL_MAX_EXACT, n, large)
                accs = [jnp.full((BIAS_ROWS, blk), table[h][-1], F32) for h in range(heads)]
                for i, b in enumerate(range(b_lo, b_hi)):
                    mask = bucket == b
                    accs = [jnp.where(mask, table[h][i], accs[h]) for h in range(heads)]
                for h in range(heads):
                    val = jnp.where(d >= 0, accs[h], NEG) if dist == 0 else accs[h]
                    o_ref[h, 0, pl.ds(r0, BIAS_ROWS), :] = val
                return carry

            lax.fori_loop(0, blk // BIAS_ROWS, rows, 0)


def rel_bias_tiles(rel_table, ndist, blk):
    heads = rel_table.shape[0]
    return pl.pallas_call(
        functools.partial(_bias_tiles_kernel, heads=heads, blk=blk, ndist=ndist),
        out_shape=jax.ShapeDtypeStruct((heads, ndist, blk, blk), F32),
        grid=(ndist,),
        in_specs=[pl.BlockSpec(memory_space=pltpu.SMEM)],
        out_specs=pl.BlockSpec((heads, 1, blk, blk), lambda d: (0, d, 0, 0)),
        compiler_params=_params("parallel"),
        name="rel_bias_tiles",
    )(rel_table.reshape(-1))


MOBA_VT_ROWS = HEAD_DIM + 8


def _moba_kernel(*refs, ncast, nb, blk, topk, scale):
    q_ref, k_ref, v_ref, bias_ref = refs[:4]
    o_ref = refs[4 + ncast]
    kaug_ref, vt_ref, kbar_ref, s_ref = refs[5 + 2 * ncast:]
    _cast_slabs(refs[4:4 + ncast], refs[5 + ncast:5 + 2 * ncast])
    qi = pl.program_id(2)
    nstream, tq, dh = q_ref.shape
    seq = k_ref.shape[1]
    nbp = kbar_ref.shape[1]

    @pl.when(qi == 0)
    def _():
        row_blk = lax.broadcasted_iota(jnp.int32, (seq, dh), 0) // blk
        col = lax.broadcasted_iota(jnp.int32, (seq, dh), 1)
        onehot = (row_blk == col).astype(BF16)
        ones_rows = (lax.broadcasted_iota(jnp.int32, (MOBA_VT_ROWS - dh, tq), 0) == 0).astype(BF16)
        for r in range(nstream):
            kaug_ref[r, :, 0:dh] = (k_ref[r].astype(F32) * (scale * LOG2E)).astype(BF16)
            kaug_ref[r, :, dh:] = onehot
            for t in range(seq // tq):
                vt_ref[r, t, 0:dh, :] = v_ref[r, t * tq:(t + 1) * tq, :].astype(F32).T.astype(BF16)
                vt_ref[r, t, dh:, :] = ones_rows
            kbar_ref[r] = jnp.zeros(kbar_ref.shape[1:], F32)
            for n in range(nb):
                kblk = k_ref[r, n * blk:(n + 1) * blk, :].astype(F32)
                kbar_ref[r, n:n + 1, :] = jnp.mean(kblk, axis=0, keepdims=True)

    blk_id = lax.broadcasted_iota(jnp.int32, (nbp, tq), 0)
    own = qi * (tq // blk) + lax.broadcasted_iota(jnp.int32, (nbp, tq), 1) // blk
    past = blk_id < own

    def augmented_query(r):
        q = q_ref[r]
        kbar = kbar_ref[r]
        kbar_hi = kbar.astype(BF16)
        kbar_lo = (kbar - kbar_hi.astype(F32)).astype(BF16)
        gate = (lax.dot_general(kbar_hi, q, _NT, preferred_element_type=F32)
                + lax.dot_general(kbar_lo, q, _NT, preferred_element_type=F32))
        gate = jnp.where(past, gate, -jnp.inf)
        rank = jnp.zeros((nbp, tq), jnp.int32)
        for m in range(nb - 1):
            gm = gate[m:m + 1, :]
            before = (gm > gate) | ((gm == gate) & (m < blk_id))
            rank = rank + before.astype(jnp.int32)
        chosen = (past & (rank < topk)) | (blk_id == own)
        pen_t = jnp.where(chosen, 0.0, NEG)
        pen_t = jnp.concatenate([pen_t, jnp.zeros((dh - nbp, tq), F32)], axis=0)
        return jnp.concatenate([q, pen_t.T.astype(BF16)], axis=1)

    q_aug = [augmented_query(r) for r in range(nstream)]

    def produce(r, dist):
        start = pl.multiple_of((qi - dist) * tq, tq)
        s = lax.dot_general(kaug_ref[r, pl.ds(start, tq), :], q_aug[r], _NT,
                            preferred_element_type=F32) + bias_ref[dist]
        s_ref[dist % 2, r] = s
        return jnp.max(s, axis=0, keepdims=True)

    def absorb(r, dist, s_max, m_i, acc):
        m_new = jnp.maximum(m_i, s_max)
        p = jnp.exp2(s_ref[dist % 2, r] - m_new)
        acc = (jnp.exp2(m_i - m_new) * acc
               + jnp.dot(vt_ref[r, qi - dist], p.astype(BF16), preferred_element_type=F32))
        return m_new, acc

    def body(dist, carry):
        done = [absorb(r, dist, *carry[r]) for r in range(nstream)]
        return tuple((produce(r, dist + 1),) + done[r] for r in range(nstream))

    init = tuple((produce(r, 0), jnp.full((1, tq), NEG, F32), jnp.zeros((MOBA_VT_ROWS, tq), F32))
                 for r in range(nstream))
    last = lax.fori_loop(0, qi, body, init)
    for r in range(nstream):
        _, acc = absorb(r, qi, *last[r])
        o_ref[r] = (acc[0:dh] / acc[dh:dh + 1]).T.astype(o_ref.dtype)


def moba_attention(proj3, q_off, k_off, v_off, bias_tiles, heads, *, nstream=4, casts=()):
    B, S, _ = proj3.shape
    blk, dh = MOBA_BLOCK, HEAD_DIM
    tq = bias_tiles.shape[-1]
    assert S % tq == 0 and tq % blk == 0 and q_off % dh == 0 and k_off % dh == 0 and v_off % dh == 0
    nstream = nstream if B % nstream == 0 else 1
    nb = S // blk
    nbp = -(-nb // 8) * 8
    assert nbp <= dh, "block one-hot columns must fit beside the keys"
    topk = min(MOBA_TOPK, nb)
    qb, kb_, vb_ = q_off // dh, k_off // dh, v_off // dh
    grid = (heads, B // nstream, S // tq)
    c_ops, c_in, c_shapes, c_out, c_views = _cast_plan(casts, grid)
    kern = functools.partial(_moba_kernel, ncast=len(casts), nb=nb, blk=blk, topk=topk, scale=dh ** -0.5)
    out, *cast = pl.pallas_call(
        kern,
        out_shape=[jax.ShapeDtypeStruct((B, S, heads * dh), BF16)] + c_shapes,
        grid=grid,
        in_specs=[pl.BlockSpec((nstream, tq, dh), lambda h, b, i: (b, i, qb + h)),
                  pl.BlockSpec((nstream, S, dh), lambda h, b, i: (b, 0, kb_ + h),
                               pipeline_mode=pl.Buffered(1)),
                  pl.BlockSpec((nstream, S, dh), lambda h, b, i: (b, 0, vb_ + h),
                               pipeline_mode=pl.Buffered(1)),
                  pl.BlockSpec((None, S // tq, tq, tq), lambda h, b, i: (h, 0, 0, 0),
                               pipeline_mode=pl.Buffered(1))] + c_in,
        out_specs=[pl.BlockSpec((nstream, tq, dh), lambda h, b, i: (b, i, h))] + c_out,
        scratch_shapes=[pltpu.VMEM((nstream, S, 2 * dh), BF16),
                        pltpu.VMEM((nstream, S // tq, MOBA_VT_ROWS, tq), BF16),
                        pltpu.VMEM((nstream, nbp, dh), F32),
                        pltpu.VMEM((2, nstream, tq, tq), F32)],
        compiler_params=_params("parallel", "parallel", "arbitrary"),
        name="moba_attention",
    )(proj3, proj3, proj3, bias_tiles, *c_ops)
    return out, [c.reshape(v) for c, v in zip(cast, c_views)]


def _suffix_sum_matrix(bk):
    r = lax.broadcasted_iota(jnp.int32, (2 * bk, 2 * bk), 0) % bk
    c = lax.broadcasted_iota(jnp.int32, (2 * bk, 2 * bk), 1)
    return ((c >= bk) | (r > c)).astype(BF16)


def _sb_kernel(*refs, ncast, scale):
    q_ref, k_ref, v_ref = refs[:3]
    o_ref = refs[3 + ncast]
    _cast_slabs(refs[3:3 + ncast], refs[4 + ncast:4 + 2 * ncast])
    qi = pl.program_id(2)
    nbatch, tq, width = q_ref.shape
    dh = HEAD_DIM
    half = tq // 2
    sums = _suffix_sum_matrix(half)
    streams = [(r, c0) for r in range(nbatch) for c0 in range(0, width, dh)]
    nstream = len(streams)
    q2 = [(q_ref[r, :, c0:c0 + dh].astype(F32) * (scale * LOG2E)).astype(BF16) for r, c0 in streams]

    def tile(s, j, run, acc, valid):
        r, c0 = streams[s]
        start = pl.multiple_of(j * tq, tq)
        z = lax.dot_general(q2[s], k_ref[r, pl.ds(start, tq), c0:c0 + dh], _NT,
                            preferred_element_type=F32)
        sp = jnp.maximum(z, 0.0) + jnp.log2(1.0 + jnp.exp2(-jnp.abs(z)))
        log_1m = -sp
        if valid is not None:
            log_1m = jnp.where(valid, log_1m, 0.0)
        hi = log_1m.astype(BF16)
        lo = (log_1m - hi.astype(F32)).astype(BF16)
        cs_b = jnp.dot(jnp.concatenate([hi[:, half:], lo[:, half:]], axis=1), sums,
                       preferred_element_type=F32)
        cs_a = jnp.dot(jnp.concatenate([hi[:, :half], lo[:, :half]], axis=1), sums,
                       preferred_element_type=F32)
        run_b = run + cs_b[:, half:]
        after = jnp.concatenate([run_b + cs_a[:, :half], run + cs_b[:, :half]], axis=1)
        a = jnp.exp2((z - sp) + after)
        if valid is not None:
            a = jnp.where(valid, a, 0.0)
        acc = acc + jnp.dot(a.astype(BF16), v_ref[r, pl.ds(start, tq), c0:c0 + dh],
                            preferred_element_type=F32)
        return run_b + cs_a[:, half:], acc

    row = lax.broadcasted_iota(jnp.int32, (tq, tq), 0)
    col = lax.broadcasted_iota(jnp.int32, (tq, tq), 1)
    no_prev = jnp.where(qi > 0, 0.0, 4.0 * EXP2_ZERO_BELOW)
    state = []
    for s in range(nstream):
        run, acc = tile(s, qi, jnp.zeros((tq, half), F32), jnp.zeros((tq, dh), F32), col < row)
        state.append(tile(s, jnp.maximum(qi - 1, 0), run + no_prev, acc, None))

    def run_max(state):
        return functools.reduce(jnp.maximum, [jnp.max(run) for run, _ in state])

    def cond(carry):
        j, _, top = carry
        return (j >= 0) & (top > EXP2_ZERO_BELOW)

    def body(carry):
        j, state, _ = carry
        state = tuple(tile(s, j, run, acc, None) for s, (run, acc) in enumerate(state))
        return j - 1, state, run_max(state)

    _, state, _ = lax.while_loop(cond, body, (qi - 2, tuple(state), run_max(state)))
    for (r, c0), (_, acc) in zip(streams, state):
        o_ref[r, :, c0:c0 + dh] = acc.astype(o_ref.dtype)


def stick_breaking_attention(proj3, q_off, k_off, v_off, heads, *, tq=2 * V7X_LANES, nbatch=4, nhead=2,
                             casts=()):
    B, S, _ = proj3.shape
    dh = HEAD_DIM
    assert tq == 2 * V7X_LANES, "suffix sums work on 128-key halves of a tile"
    nbatch = nbatch if B % nbatch == 0 else 1
    width = nhead * dh
    assert S % tq == 0 and heads % nhead == 0
    assert q_off % width == 0 and k_off % width == 0 and v_off % width == 0
    qb, kb_, vb_ = q_off // width, k_off // width, v_off // width
    grid = (heads // nhead, B // nbatch, S // tq)
    c_ops, c_in, c_shapes, c_out, c_views = _cast_plan(casts, grid)
    out, *cast = pl.pallas_call(
        functools.partial(_sb_kernel, ncast=len(casts), scale=dh ** -0.5),
        out_shape=[jax.ShapeDtypeStruct((B, S, heads * dh), BF16)] + c_shapes,
        grid=grid,
        in_specs=[pl.BlockSpec((nbatch, tq, width), lambda h, b, i: (b, i, qb + h)),
                  pl.BlockSpec((nbatch, S, width), lambda h, b, i: (b, 0, kb_ + h)),
                  pl.BlockSpec((nbatch, S, width), lambda h, b, i: (b, 0, vb_ + h))] + c_in,
        out_specs=[pl.BlockSpec((nbatch, tq, width), lambda h, b, i: (b, i, h))] + c_out,
        compiler_params=_params("parallel", "parallel", "parallel"),
        name="stick_breaking_attention",
    )(proj3, proj3, proj3, *c_ops)
    return out, [c.reshape(v) for c, v in zip(cast, c_views)]


def kernel(x, norm_mix, norm_ffn, w_in, w_pool, pool_scale, w_br_a, w_br_b, w_br_c, w_out, w_gate,
           w_up, w_down, rel_bias, norm_final):
    B, S, D = x.shape
    depth = w_in.shape[0]
    T = B * S
    moba_w = MOBA_HEADS * HEAD_DIM
    sb_w = SB_HEADS * HEAD_DIM
    pool_w = len(POOL_WINDOWS) * POOL_GROUP
    off_qa, off_ka, off_va = 0, moba_w, 2 * moba_w
    off_u = 3 * moba_w
    off_qs = off_u + pool_w
    off_ks, off_vs = off_qs + sb_w, off_qs + 2 * sb_w
    off_gates = off_qs + 3 * sb_w

    bias_tiles = rel_bias_tiles(rel_bias, S // MOBA_QTILE, MOBA_QTILE)
    w_pool, w_br_a, w_br_b, w_br_c, w_out = (
        w.astype(BF16) for w in (w_pool, w_br_a, w_br_b, w_br_c, w_out))
    h = x.reshape(T, D)
    w_in_l, layer_in = w_in, 0
    for l in range(depth):
        proj = norm_matmul(h, norm_mix[l], w_in_l, layer_in)
        proj3 = proj.reshape(B, S, proj.shape[1])
        moba_casts = [(w_down, l)] + ([(w_in, l + 1)] if l + 1 < depth else [])
        ya, moba_cast = moba_attention(proj3, off_qa, off_ka, off_va, bias_tiles, MOBA_HEADS,
                                       casts=moba_casts)
        yb = pool_mixer(proj, off_u, w_pool, l, pool_scale[l], S)
        yc, (w_gate_l, w_up_l) = stick_breaking_attention(proj3, off_qs, off_ks, off_vs, SB_HEADS,
                                                          casts=[(w_gate, l), (w_up, l)])
        m = merge_branches(ya.reshape(T, moba_w), yb, yc.reshape(T, sb_w), proj, off_gates,
                           w_br_a, w_br_b, w_br_c, l)
        h = matmul_res(m, w_out, l, h, tn=1024)
        act = ffn_up(h, norm_ffn[l], w_gate_l, w_up_l, 0)
        h = matmul_res(act, moba_cast[0], 0, h)
        if l + 1 < depth:
            w_in_l, layer_in = moba_cast[1], 0
    return final_norm(h, norm_final).reshape(B, S, D)
```

```python
import functools
import math

import jax
import jax.numpy as jnp
from jax import lax
from jax.experimental import pallas as pl
from jax.experimental.pallas import tpu as pltpu

F32 = jnp.float32
BF16 = jnp.bfloat16

HEAD_DIM = 128
MOBA_HEADS = 8
MOBA_BLOCK = 256
MOBA_TOPK = 3
MOBA_QTILE = 512
SB_HEADS = 8
POOL_WINDOWS = (2, 4, 8, 16)
POOL_GROUP = 256
POOL_HALO = 16
REL_BUCKETS = 32
REL_MAX_EXACT = 16
REL_MAX_DIST = 2048
EPS = 1e-6
NEG = -1e30
LOG2E = math.log2(math.e)
EXP2_ZERO_BELOW = -152.0

V7X_LANES = 128
V7X_VMEM_LIMIT_BYTES = 56 * 1024 * 1024

_NT = (((1,), (1,)), ((), ()))


def _params(*semantics):
    return pltpu.CompilerParams(dimension_semantics=semantics,
                                vmem_limit_bytes=V7X_VMEM_LIMIT_BYTES)


def _pick_tile(n, want, align):
    t = min(want, n)
    while t > align and (n % t or t % align):
        t -= align
    assert n % t == 0 and t % align == 0, (n, want, align)
    return t


def _rms_normalize(x, g):
    ms = jnp.mean(x * x, axis=-1, keepdims=True)
    return x * lax.rsqrt(ms + EPS) * g


NORM_CHUNK = 256


def _norm_matmul_kernel(x_ref, g_ref, w_ref, o_ref, xn_ref):
    @pl.when(pl.program_id(1) == 0)
    def _():
        w = w_ref[...].astype(BF16)
        for r0 in range(0, x_ref.shape[0], NORM_CHUNK):
            rows = pl.ds(r0, NORM_CHUNK)
            xn = _rms_normalize(x_ref[rows, :], g_ref[...]).astype(BF16)
            xn_ref[rows, :] = xn
            o_ref[rows, :] = jnp.dot(xn, w, preferred_element_type=F32).astype(o_ref.dtype)

    @pl.when(pl.program_id(1) > 0)
    def _():
        o_ref[...] = jnp.dot(xn_ref[...], w_ref[...].astype(BF16),
                             preferred_element_type=F32).astype(o_ref.dtype)


def norm_matmul(x, g, w, layer, *, out_dtype=BF16, tm=1024, tn=1024):
    T, K = x.shape
    N = w.shape[2]
    tm = _pick_tile(T, tm, 16)
    tn = _pick_tile(N, tn, V7X_LANES)
    return pl.pallas_call(
        _norm_matmul_kernel,
        out_shape=jax.ShapeDtypeStruct((T, N), out_dtype),
        grid=(T // tm, N // tn),
        in_specs=[pl.BlockSpec((tm, K), lambda i, j: (i, 0)),
                  pl.BlockSpec((1, K), lambda i, j: (0, 0)),
                  pl.BlockSpec((None, K, tn), lambda i, j: (layer, 0, j))],
        out_specs=pl.BlockSpec((tm, tn), lambda i, j: (i, j)),
        scratch_shapes=[pltpu.VMEM((tm, K), BF16)],
        compiler_params=_params("parallel", "arbitrary"),
        name="norm_matmul",
    )(x, g.reshape(1, K), w)


def _ffn_up_kernel(x_ref, g_ref, wg_ref, wu_ref, o_ref, xn_ref):
    def swiglu(xn, wg, wu):
        a = jnp.dot(xn, wg, preferred_element_type=F32)
        b = jnp.dot(xn, wu, preferred_element_type=F32)
        return (a * jax.nn.sigmoid(a) * b).astype(o_ref.dtype)

    @pl.when(pl.program_id(1) == 0)
    def _():
        wg, wu = wg_ref[...].astype(BF16), wu_ref[...].astype(BF16)
        for r0 in range(0, x_ref.shape[0], NORM_CHUNK):
            rows = pl.ds(r0, NORM_CHUNK)
            xn = _rms_normalize(x_ref[rows, :], g_ref[...]).astype(BF16)
            xn_ref[rows, :] = xn
            o_ref[rows, :] = swiglu(xn, wg, wu)

    @pl.when(pl.program_id(1) > 0)
    def _():
        o_ref[...] = swiglu(xn_ref[...], wg_ref[...].astype(BF16), wu_ref[...].astype(BF16))


def ffn_up(x, g, wg, wu, layer, *, tm=1024, tn=512):
    T, K = x.shape
    N = wg.shape[2]
    tm = _pick_tile(T, tm, 16)
    tn = _pick_tile(N, tn, V7X_LANES)
    return pl.pallas_call(
        _ffn_up_kernel,
        out_shape=jax.ShapeDtypeStruct((T, N), BF16),
        grid=(T // tm, N // tn),
        in_specs=[pl.BlockSpec((tm, K), lambda i, j: (i, 0)),
                  pl.BlockSpec((1, K), lambda i, j: (0, 0)),
                  pl.BlockSpec((None, K, tn), lambda i, j: (layer, 0, j)),
                  pl.BlockSpec((None, K, tn), lambda i, j: (layer, 0, j))],
        out_specs=pl.BlockSpec((tm, tn), lambda i, j: (i, j)),
        scratch_shapes=[pltpu.VMEM((tm, K), BF16)],
        compiler_params=_params("parallel", "arbitrary"),
        name="ffn_up",
    )(x, g.reshape(1, K), wg, wu)


def _matmul_res_kernel(x_ref, w_ref, r_ref, o_ref):
    o_ref[...] = r_ref[...] + jnp.dot(x_ref[...], w_ref[...], preferred_element_type=F32)


def matmul_res(x, w, layer, res, *, tm=1024, tn=512):
    T, K = x.shape
    N = w.shape[2]
    tm = _pick_tile(T, tm, 16)
    tn = _pick_tile(N, tn, V7X_LANES)
    return pl.pallas_call(
        _matmul_res_kernel,
        out_shape=jax.ShapeDtypeStruct((T, N), F32),
        grid=(T // tm, N // tn),
        in_specs=[pl.BlockSpec((tm, K), lambda i, j: (i, 0)),
                  pl.BlockSpec((None, K, tn), lambda i, j: (layer, 0, j)),
                  pl.BlockSpec((tm, tn), lambda i, j: (i, j))],
        out_specs=pl.BlockSpec((tm, tn), lambda i, j: (i, j)),
        compiler_params=_params("parallel", "parallel"),
        name="matmul_res",
    )(x, w, res)


def _merge_kernel(ya_ref, yb_ref, yc_ref, ga_ref, gb_ref, gc_ref, wa_ref, wb_ref, wc_ref, o_ref):
    def branch(y_ref, g_ref, w_ref):
        gate = jax.nn.sigmoid(g_ref[...].astype(F32))
        return gate * jnp.dot(y_ref[...], w_ref[...], preferred_element_type=F32)

    m = branch(ya_ref, ga_ref, wa_ref) + branch(yb_ref, gb_ref, wb_ref) + branch(yc_ref, gc_ref, wc_ref)
    o_ref[...] = m.astype(o_ref.dtype)


def merge_branches(ya, yb, yc, proj, gate_off, wa, wb, wc, layer, *, tm=1024, tn=1024):
    T, K = ya.shape
    N = wa.shape[2]
    tm = _pick_tile(T, tm, 16)
    tn = _pick_tile(N, tn, V7X_LANES)
    assert gate_off % tn == 0
    y_spec = pl.BlockSpec((tm, K), lambda i, j: (i, 0))
    w_spec = pl.BlockSpec((None, K, tn), lambda i, j: (layer, 0, j))

    def gate_spec(branch):
        base = (gate_off + branch * N) // tn
        return pl.BlockSpec((tm, tn), lambda i, j: (i, base + j))

    return pl.pallas_call(
        _merge_kernel,
        out_shape=jax.ShapeDtypeStruct((T, N), BF16),
        grid=(T // tm, N // tn),
        in_specs=[y_spec, y_spec, y_spec, gate_spec(0), gate_spec(1), gate_spec(2),
                  w_spec, w_spec, w_spec],
        out_specs=pl.BlockSpec((tm, tn), lambda i, j: (i, j)),
        compiler_params=_params("parallel", "parallel"),
        name="merge_branches",
    )(ya, yb, yc, proj, proj, proj, wa, wb, wc)


def _final_norm_kernel(x_ref, g_ref, o_ref):
    o_ref[...] = _rms_normalize(x_ref[...], g_ref[...])


def final_norm(x, g, *, tm=512):
    T, K = x.shape
    tm = _pick_tile(T, tm, 8)
    return pl.pallas_call(
        _final_norm_kernel,
        out_shape=jax.ShapeDtypeStruct((T, K), F32),
        grid=(T // tm,),
        in_specs=[pl.BlockSpec((tm, K), lambda i: (i, 0)),
                  pl.BlockSpec((1, K), lambda i: (0, 0))],
        out_specs=pl.BlockSpec((tm, K), lambda i: (i, 0)),
        compiler_params=_params("parallel"),
        name="final_norm",
    )(x, g.reshape(1, K))


POOL_PAD = 8


def _pool_kernel(u_ref, halo_ref, w_ref, sc_ref, o_ref, ext_ref, stage_ref, *, tm, tiles_per_seq):
    t_in_seq = pl.program_id(0) % tiles_per_seq
    base = POOL_PAD + POOL_HALO
    rows = tm + POOL_HALO
    zeros = jnp.zeros((POOL_PAD, ext_ref.shape[1]), F32)
    ext_ref[0:POOL_PAD, :] = zeros
    ext_ref[POOL_PAD:base, :] = jnp.where(t_in_seq == 0, 0.0, halo_ref[...].astype(F32))
    ext_ref[base:, :] = u_ref[...].astype(F32)
    stage_ref[0:POOL_PAD, :] = zeros[:, 0:POOL_GROUP]
    pos = t_in_seq * tm + lax.broadcasted_iota(jnp.int32, (tm, 1), 0)
    for gi, win in enumerate(POOL_WINDOWS):
        cols = slice(gi * POOL_GROUP, (gi + 1) * POOL_GROUP)
        x = ext_ref[base:, cols]
        cur = ext_ref[pl.ds(POOL_PAD, rows), cols] + ext_ref[pl.ds(POOL_PAD - 1, rows), cols]
        span = 2
        while span < win:
            stage_ref[pl.ds(POOL_PAD, rows), :] = cur
            cur = cur + stage_ref[pl.ds(POOL_PAD - span, rows), :]
            span *= 2
        assert span == win <= 2 * POOL_PAD, "pooling windows must be powers of two up to 2 * POOL_PAD"
        total = cur[POOL_HALO:]
        cnt = jnp.minimum(pos + 1, win).astype(F32)
        pooled = total / cnt - x
        y = jnp.dot(pooled.astype(BF16), w_ref[gi], preferred_element_type=F32)
        o_ref[:, cols] = (y * sc_ref[:, cols]).astype(o_ref.dtype)


def pool_mixer(proj, u_off, w_grp, layer, scale, seq_len, *, tm=1024):
    T = proj.shape[0]
    width = len(POOL_WINDOWS) * POOL_GROUP
    tm = _pick_tile(seq_len, tm, POOL_HALO)
    assert u_off % width == 0 and max(POOL_WINDOWS) <= POOL_HALO
    ublk = u_off // width
    rows_per_tile = tm // POOL_HALO
    return pl.pallas_call(
        functools.partial(_pool_kernel, tm=tm, tiles_per_seq=seq_len // tm),
        out_shape=jax.ShapeDtypeStruct((T, width), BF16),
        grid=(T // tm,),
        in_specs=[pl.BlockSpec((tm, width), lambda i: (i, ublk)),
                  pl.BlockSpec((POOL_HALO, width),
                               lambda i: (jnp.maximum(i * rows_per_tile - 1, 0), ublk)),
                  pl.BlockSpec((None,) + w_grp.shape[1:], lambda i: (layer, 0, 0, 0)),
                  pl.BlockSpec((1, width), lambda i: (0, 0))],
        out_specs=pl.BlockSpec((tm, width), lambda i: (i, 0)),
        scratch_shapes=[pltpu.VMEM((POOL_PAD + POOL_HALO + tm, width), F32),
                        pltpu.VMEM((POOL_PAD + POOL_HALO + tm, POOL_GROUP), F32)],
        compiler_params=_params("parallel"),
        name="pool_mixer",
    )(proj, proj, w_grp, scale.reshape(1, width))


BF16_ROWS = 16


def _cast_plan(casts, grid):
    nsteps = grid[0] * grid[1] * grid[2]
    step = lambda h, b, i: (h * grid[1] + b) * grid[2] + i
    operands, in_specs, out_shapes, out_specs, shapes = [], [], [], [], []
    for w, layer in casts:
        depth, K, N = w.shape
        share = 1
        while (K * share) % (nsteps * BF16_ROWS) and share < nsteps:
            share *= 2
        assert (K * share) % (nsteps * BF16_ROWS) == 0 and nsteps % share == 0, (w.shape, nsteps)
        rows = K * share // nsteps
        operands.append(w)
        in_specs.append(pl.BlockSpec((None, rows, N), lambda h, b, i, layer=layer, share=share:
                                     (layer, step(h, b, i) // share, 0)))
        out_shapes.append(jax.ShapeDtypeStruct((K, N), BF16))
        out_specs.append(pl.BlockSpec((rows, N), lambda h, b, i, share=share: (step(h, b, i) // share, 0)))
        shapes.append((1, K, N))
    return operands, in_specs, out_shapes, out_specs, shapes


def _cast_slabs(src_refs, dst_refs):
    for src, dst in zip(src_refs, dst_refs):
        dst[...] = src[...].astype(BF16)


def _bucket_range(d_lo, d_hi):
    def bucket(d):
        if d < REL_MAX_EXACT:
            return d
        big = REL_MAX_EXACT + int(math.log(d / REL_MAX_EXACT) / math.log(REL_MAX_DIST / REL_MAX_EXACT)
                                  * (REL_BUCKETS - REL_MAX_EXACT))
        return min(big, REL_BUCKETS - 1)
    return max(bucket(max(d_lo, 0)) - 1, 0), min(bucket(d_hi) + 1, REL_BUCKETS - 1)


BIAS_ROWS = 8


def _bias_tiles_kernel(tab_ref, o_ref, *, heads, blk, ndist):
    key0 = lax.broadcasted_iota(jnp.int32, (BIAS_ROWS, blk), 0)
    qry = lax.broadcasted_iota(jnp.int32, (BIAS_ROWS, blk), 1)
    for dist in range(ndist):
        @pl.when(pl.program_id(0) == dist)
        def _(dist=dist):
            b_lo, b_hi = _bucket_range(dist * blk - (blk - 1), dist * blk + (blk - 1))
            table = [[tab_ref[h * REL_BUCKETS + b] * LOG2E for b in range(b_lo, b_hi + 1)]
                     for h in range(heads)]

            def rows(c, carry):
                r0 = pl.multiple_of(c * BIAS_ROWS, BIAS_ROWS)
                d = dist * blk + qry - (key0 + r0)
                n = jnp.maximum(d, 0)
                nf = jnp.maximum(n, 1).astype(F32)
                large = REL_MAX_EXACT + (jnp.log(nf / REL_MAX_EXACT)
                                         / math.log(REL_MAX_DIST / REL_MAX_EXACT)
                                         * (REL_BUCKETS - REL_MAX_EXACT)).astype(jnp.int32)
                large = jnp.minimum(large, REL_BUCKETS - 1)
                bucket = jnp.where(n < REL_MAX_EXACT, n, large)
                accs = [jnp.full((BIAS_ROWS, blk), table[h][-1], F32) for h in range(heads)]
                for i, b in enumerate(range(b_lo, b_hi)):
                    mask = bucket == b
                    accs = [jnp.where(mask, table[h][i], accs[h]) for h in range(heads)]
                for h in range(heads):
                    val = jnp.where(d >= 0, accs[h], NEG) if dist == 0 else accs[h]
                    o_ref[h, 0, pl.ds(r0, BIAS_ROWS), :] = val
                return carry

            lax.fori_loop(0, blk // BIAS_ROWS, rows, 0)


def rel_bias_tiles(rel_table, ndist, blk):
    heads = rel_table.shape[0]
    return pl.pallas_call(
        functools.partial(_bias_tiles_kernel, heads=heads, blk=blk, ndist=ndist),
        out_shape=jax.ShapeDtypeStruct((heads, ndist, blk, blk), F32),
        grid=(ndist,),
        in_specs=[pl.BlockSpec(memory_space=pltpu.SMEM)],
        out_specs=pl.BlockSpec((heads, 1, blk, blk), lambda d: (0, d, 0, 0)),
        compiler_params=_params("parallel"),
        name="rel_bias_tiles",
    )(rel_table.reshape(-1))


MOBA_VT_ROWS = HEAD_DIM + 8


def _moba_kernel(*refs, ncast, nb, blk, topk, scale):
    q_ref, k_ref, v_ref, bias_ref = refs[:4]
    o_ref = refs[4 + ncast]
    kaug_ref, vt_ref, kbar_ref, s_ref = refs[5 + 2 * ncast:]
    _cast_slabs(refs[4:4 + ncast], refs[5 + ncast:5 + 2 * ncast])
    qi = pl.program_id(2)
    nstream, tq, dh = q_ref.shape
    seq = k_ref.shape[1]
    nbp = kbar_ref.shape[1]

    @pl.when(qi == 0)
    def _():
        row_blk = lax.broadcasted_iota(jnp.int32, (seq, dh), 0) // blk
        col = lax.broadcasted_iota(jnp.int32, (seq, dh), 1)
        onehot = (row_blk == col).astype(BF16)
        ones_rows = (lax.broadcasted_iota(jnp.int32, (MOBA_VT_ROWS - dh, tq), 0) == 0).astype(BF16)
        for r in range(nstream):
            kaug_ref[r, :, 0:dh] = (k_ref[r].astype(F32) * (scale * LOG2E)).astype(BF16)
            kaug_ref[r, :, dh:] = onehot
            for t in range(seq // tq):
                vt_ref[r, t, 0:dh, :] = v_ref[r, t * tq:(t + 1) * tq, :].astype(F32).T.astype(BF16)
                vt_ref[r, t, dh:, :] = ones_rows
            kbar_ref[r] = jnp.zeros(kbar_ref.shape[1:], F32)
            for n in range(nb):
                kblk = k_ref[r, n * blk:(n + 1) * blk, :].astype(F32)
                kbar_ref[r, n:n + 1, :] = jnp.mean(kblk, axis=0, keepdims=True)

    blk_id = lax.broadcasted_iota(jnp.int32, (nbp, tq), 0)
    own = qi * (tq // blk) + lax.broadcasted_iota(jnp.int32, (nbp, tq), 1) // blk
    past = blk_id < own

    def augmented_query(r):
        q = q_ref[r]
        kbar = kbar_ref[r]
        kbar_hi = kbar.astype(BF16)
        kbar_lo = (kbar - kbar_hi.astype(F32)).astype(BF16)
        gate = (lax.dot_general(kbar_hi, q, _NT, preferred_element_type=F32)
                + lax.dot_general(kbar_lo, q, _NT, preferred_element_type=F32))
        gate = jnp.where(past, gate, -jnp.inf)
        rank = jnp.zeros((nbp, tq), jnp.int32)
        for m in range(nb - 1):
            gm = gate[m:m + 1, :]
            before = (gm > gate) | ((gm == gate) & (m < blk_id))
            rank = rank + before.astype(jnp.int32)
        chosen = (past & (rank < topk)) | (blk_id == own)
        pen_t = jnp.where(chosen, 0.0, NEG)
        pen_t = jnp.concatenate([pen_t, jnp.zeros((dh - nbp, tq), F32)], axis=0)
        return jnp.concatenate([q, pen_t.T.astype(BF16)], axis=1)

    q_aug = [augmented_query(r) for r in range(nstream)]

    def produce(r, dist):
        start = pl.multiple_of((qi - dist) * tq, tq)
        s = lax.dot_general(kaug_ref[r, pl.ds(start, tq), :], q_aug[r], _NT,
                            preferred_element_type=F32) + bias_ref[dist]
        s_ref[dist % 2, r] = s
        return jnp.max(s, axis=0, keepdims=True)

    def absorb(r, dist, s_max, m_i, acc):
        m_new = jnp.maximum(m_i, s_max)
        p = jnp.exp2(s_ref[dist % 2, r] - m_new)
        acc = (jnp.exp2(m_i - m_new) * acc
               + jnp.dot(vt_ref[r, qi - dist], p.astype(BF16), preferred_element_type=F32))
        return m_new, acc

    def body(dist, carry):
        done = [absorb(r, dist, *carry[r]) for r in range(nstream)]
        return tuple((produce(r, dist + 1),) + done[r] for r in range(nstream))

    init = tuple((produce(r, 0), jnp.full((1, tq), NEG, F32), jnp.zeros((MOBA_VT_ROWS, tq), F32))
                 for r in range(nstream))
    last = lax.fori_loop(0, qi, body, init)
    for r in range(nstream):
        _, acc = absorb(r, qi, *last[r])
        o_ref[r] = (acc[0:dh] / acc[dh:dh + 1]).T.astype(o_ref.dtype)


def moba_attention(proj3, q_off, k_off, v_off, bias_tiles, heads, *, nstream=4, casts=()):
    B, S, _ = proj3.shape
    blk, dh = MOBA_BLOCK, HEAD_DIM
    tq = bias_tiles.shape[-1]
    assert S % tq == 0 and tq % blk == 0 and q_off % dh == 0 and k_off % dh == 0 and v_off % dh == 0
    nstream = nstream if B % nstream == 0 else 1
    nb = S // blk
    nbp = -(-nb // 8) * 8
    assert nbp <= dh, "block one-hot columns must fit beside the keys"
    topk = min(MOBA_TOPK, nb)
    qb, kb_, vb_ = q_off // dh, k_off // dh, v_off // dh
    grid = (heads, B // nstream, S // tq)
    c_ops, c_in, c_shapes, c_out, c_views = _cast_plan(casts, grid)
    kern = functools.partial(_moba_kernel, ncast=len(casts), nb=nb, blk=blk, topk=topk, scale=dh ** -0.5)
    out, *cast = pl.pallas_call(
        kern,
        out_shape=[jax.ShapeDtypeStruct((B, S, heads * dh), BF16)] + c_shapes,
        grid=grid,
        in_specs=[pl.BlockSpec((nstream, tq, dh), lambda h, b, i: (b, i, qb + h)),
                  pl.BlockSpec((nstream, S, dh), lambda h, b, i: (b, 0, kb_ + h),
                               pipeline_mode=pl.Buffered(1)),
                  pl.BlockSpec((nstream, S, dh), lambda h, b, i: (b, 0, vb_ + h),
                               pipeline_mode=pl.Buffered(1)),
                  pl.BlockSpec((None, S // tq, tq, tq), lambda h, b, i: (h, 0, 0, 0),
                               pipeline_mode=pl.Buffered(1))] + c_in,
        out_specs=[pl.BlockSpec((nstream, tq, dh), lambda h, b, i: (b, i, h))] + c_out,
        scratch_shapes=[pltpu.VMEM((nstream, S, 2 * dh), BF16),
                        pltpu.VMEM((nstream, S // tq, MOBA_VT_ROWS, tq), BF16),
                        pltpu.VMEM((nstream, nbp, dh), F32),
                        pltpu.VMEM((2, nstream, tq, tq), F32)],
        compiler_params=_params("parallel", "parallel", "arbitrary"),
        name="moba_attention",
    )(proj3, proj3, proj3, bias_tiles, *c_ops)
    return out, [c.reshape(v) for c, v in zip(cast, c_views)]


def _suffix_sum_matrix(bk):
    r = lax.broadcasted_iota(jnp.int32, (2 * bk, 2 * bk), 0) % bk
    c = lax.broadcasted_iota(jnp.int32, (2 * bk, 2 * bk), 1)
    return ((c >= bk) | (r > c)).astype(BF16)


def _sb_kernel(*refs, ncast, scale):
    q_ref, k_ref, v_ref = refs[:3]
    o_ref = refs[3 + ncast]
    _cast_slabs(refs[3:3 + ncast], refs[4 + ncast:4 + 2 * ncast])
    qi = pl.program_id(2)
    nbatch, tq, width = q_ref.shape
    dh = HEAD_DIM
    half = tq // 2
    sums = _suffix_sum_matrix(half)
    streams = [(r, c0) for r in range(nbatch) for c0 in range(0, width, dh)]
    nstream = len(streams)
    q2 = [(q_ref[r, :, c0:c0 + dh].astype(F32) * (scale * LOG2E)).astype(BF16) for r, c0 in streams]

    def tile(s, j, run, acc, valid):
        r, c0 = streams[s]
        start = pl.multiple_of(j * tq, tq)
        z = lax.dot_general(q2[s], k_ref[r, pl.ds(start, tq), c0:c0 + dh], _NT,
                            preferred_element_type=F32)
        sp = jnp.maximum(z, 0.0) + jnp.log2(1.0 + jnp.exp2(-jnp.abs(z)))
        log_1m = -sp
        if valid is not None:
            log_1m = jnp.where(valid, log_1m, 0.0)
        hi = log_1m.astype(BF16)
        lo = (log_1m - hi.astype(F32)).astype(BF16)
        cs_b = jnp.dot(jnp.concatenate([hi[:, half:], lo[:, half:]], axis=1), sums,
                       preferred_element_type=F32)
        cs_a = jnp.dot(jnp.concatenate([hi[:, :half], lo[:, :half]], axis=1), sums,
                       preferred_element_type=F32)
        run_b = run + cs_b[:, half:]
        after = jnp.concatenate([run_b + cs_a[:, :half], run + cs_b[:, :half]], axis=1)
        a = jnp.exp2((z - sp) + after)
        if valid is not None:
            a = jnp.where(valid, a, 0.0)
        acc = acc + jnp.dot(a.astype(BF16), v_ref[r, pl.ds(start, tq), c0:c0 + dh],
                            preferred_element_type=F32)
        return run_b + cs_a[:, half:], acc

    row = lax.broadcasted_iota(jnp.int32, (tq, tq), 0)
    col = lax.broadcasted_iota(jnp.int32, (tq, tq), 1)
    no_prev = jnp.where(qi > 0, 0.0, 4.0 * EXP2_ZERO_BELOW)
    state = []
    for s in range(nstream):
        run, acc = tile(s, qi, jnp.zeros((tq, half), F32), jnp.zeros((tq, dh), F32), col < row)
        state.append(tile(s, jnp.maximum(qi - 1, 0), run + no_prev, acc, None))

    def run_max(state):
        return functools.reduce(jnp.maximum, [jnp.max(run) for run, _ in state])

    def cond(carry):
        j, _, top = carry
        return (j >= 0) & (top > EXP2_ZERO_BELOW)

    def body(carry):
        j, state, _ = carry
        state = tuple(tile(s, j, run, acc, None) for s, (run, acc) in enumerate(state))
        return j - 1, state, run_max(state)

    _, state, _ = lax.while_loop(cond, body, (qi - 2, tuple(state), run_max(state)))
    for (r, c0), (_, acc) in zip(streams, state):
        o_ref[r, :, c0:c0 + dh] = acc.astype(o_ref.dtype)


def stick_breaking_attention(proj3, q_off, k_off, v_off, heads, *, tq=2 * V7X_LANES, nbatch=4, nhead=2,
                             casts=()):
    B, S, _ = proj3.shape
    dh = HEAD_DIM
    assert tq == 2 * V7X_LANES, "suffix sums work on 128-key halves of a tile"
    nbatch = nbatch if B % nbatch == 0 else 1
    width = nhead * dh
    assert S % tq == 0 and heads % nhead == 0
    assert q_off % width == 0 and k_off % width == 0 and v_off % width == 0
    qb, kb_, vb_ = q_off // width, k_off // width, v_off // width
    grid = (heads // nhead, B // nbatch, S // tq)
    c_ops, c_in, c_shapes, c_out, c_views = _cast_plan(casts, grid)
    out, *cast = pl.pallas_call(
        functools.partial(_sb_kernel, ncast=len(casts), scale=dh ** -0.5),
        out_shape=[jax.ShapeDtypeStruct((B, S, heads * dh), BF16)] + c_shapes,
        grid=grid,
        in_specs=[pl.BlockSpec((nbatch, tq, width), lambda h, b, i: (b, i, qb + h)),
                  pl.BlockSpec((nbatch, S, width), lambda h, b, i: (b, 0, kb_ + h)),
                  pl.BlockSpec((nbatch, S, width), lambda h, b, i: (b, 0, vb_ + h))] + c_in,
        out_specs=[pl.BlockSpec((nbatch, tq, width), lambda h, b, i: (b, i, h))] + c_out,
        compiler_params=_params("parallel", "parallel", "parallel"),
        name="stick_breaking_attention",
    )(proj3, proj3, proj3, *c_ops)
    return out, [c.reshape(v) for c, v in zip(cast, c_views)]


def kernel(x, norm_mix, norm_ffn, w_in, w_pool, pool_scale, w_br_a, w_br_b, w_br_c, w_out, w_gate,
           w_up, w_down, rel_bias, norm_final):
    B, S, D = x.shape
    depth = w_in.shape[0]
    T = B * S
    moba_w = MOBA_HEADS * HEAD_DIM
    sb_w = SB_HEADS * HEAD_DIM
    pool_w = len(POOL_WINDOWS) * POOL_GROUP
    off_qa, off_ka, off_va = 0, moba_w, 2 * moba_w
    off_u = 3 * moba_w
    off_qs = off_u + pool_w
    off_ks, off_vs = off_qs + sb_w, off_qs + 2 * sb_w
    off_gates = off_qs + 3 * sb_w

    bias_tiles = rel_bias_tiles(rel_bias, S // MOBA_QTILE, MOBA_QTILE)
    w_pool = w_pool.astype(BF16)
    h = x.reshape(T, D)
    w_in_l, layer_in = w_in, 0
    for l in range(depth):
        proj = norm_matmul(h, norm_mix[l], w_in_l, layer_in)
        proj3 = proj.reshape(B, S, proj.shape[1])
        moba_casts = [(w_down, l)] + ([(w_in, l + 1)] if l + 1 < depth else [])
        ya, moba_cast = moba_attention(proj3, off_qa, off_ka, off_va, bias_tiles, MOBA_HEADS,
                                       casts=moba_casts)
        yb = pool_mixer(proj, off_u, w_pool, l, pool_scale[l], S)
        sb_casts = [(w, l) for w in (w_gate, w_up, w_br_a, w_br_b, w_br_c, w_out)]
        yc, (w_gate_l, w_up_l, wa_l, wb_l, wc_l, w_out_l) = stick_breaking_attention(
            proj3, off_qs, off_ks, off_vs, SB_HEADS, casts=sb_casts)
        m = merge_branches(ya.reshape(T, moba_w), yb, yc.reshape(T, sb_w), proj, off_gates,
                           wa_l, wb_l, wc_l, 0)
        h = matmul_res(m, w_out_l, 0, h, tn=1024)
        act = ffn_up(h, norm_ffn[l], w_gate_l, w_up_l, 0)
        h = matmul_res(act, moba_cast[0], 0, h)
        if l + 1 < depth:
            w_in_l, layer_in = moba_cast[1], 0
    return final_norm(h, norm_final).reshape(B, S, D)
```

```python
import functools
import math

import jax
import jax.numpy as jnp
from jax import lax
from jax.experimental import pallas as pl
from jax.experimental.pallas import tpu as pltpu

F32 = jnp.float32
BF16 = jnp.bfloat16

HEAD_DIM = 128
MOBA_HEADS = 8
MOBA_BLOCK = 256
MOBA_TOPK = 3
MOBA_QTILE = 512
SB_HEADS = 8
POOL_WINDOWS = (2, 4, 8, 16)
POOL_GROUP = 256
POOL_HALO = 16
REL_BUCKETS = 32
REL_MAX_EXACT = 16
REL_MAX_DIST = 2048
EPS = 1e-6
NEG = -1e30
LOG2E = math.log2(math.e)
EXP2_ZERO_BELOW = -152.0

V7X_LANES = 128
V7X_VMEM_LIMIT_BYTES = 56 * 1024 * 1024

_NT = (((1,), (1,)), ((), ()))


def _params(*semantics):
    return pltpu.CompilerParams(dimension_semantics=semantics,
                                vmem_limit_bytes=V7X_VMEM_LIMIT_BYTES)


def _pick_tile(n, want, align):
    t = min(want, n)
    while t > align and (n % t or t % align):
        t -= align
    assert n % t == 0 and t % align == 0, (n, want, align)
    return t


def _rms_normalize(x, g):
    ms = jnp.mean(x * x, axis=-1, keepdims=True)
    return x * lax.rsqrt(ms + EPS) * g


NORM_CHUNK = 256


def _norm_matmul_kernel(x_ref, g_ref, w_ref, o_ref, xn_ref):
    @pl.when(pl.program_id(1) == 0)
    def _():
        w = w_ref[...].astype(BF16)
        for r0 in range(0, x_ref.shape[0], NORM_CHUNK):
            rows = pl.ds(r0, NORM_CHUNK)
            xn = _rms_normalize(x_ref[rows, :], g_ref[...]).astype(BF16)
            xn_ref[rows, :] = xn
            o_ref[rows, :] = jnp.dot(xn, w, preferred_element_type=F32).astype(o_ref.dtype)

    @pl.when(pl.program_id(1) > 0)
    def _():
        o_ref[...] = jnp.dot(xn_ref[...], w_ref[...].astype(BF16),
                             preferred_element_type=F32).astype(o_ref.dtype)


def norm_matmul(x, g, w, layer, *, out_dtype=BF16, tm=1024, tn=1024):
    T, K = x.shape
    N = w.shape[2]
    tm = _pick_tile(T, tm, 16)
    tn = _pick_tile(N, tn, V7X_LANES)
    return pl.pallas_call(
        _norm_matmul_kernel,
        out_shape=jax.ShapeDtypeStruct((T, N), out_dtype),
        grid=(T // tm, N // tn),
        in_specs=[pl.BlockSpec((tm, K), lambda i, j: (i, 0)),
                  pl.BlockSpec((1, K), lambda i, j: (0, 0)),
                  pl.BlockSpec((None, K, tn), lambda i, j: (layer, 0, j))],
        out_specs=pl.BlockSpec((tm, tn), lambda i, j: (i, j)),
        scratch_shapes=[pltpu.VMEM((tm, K), BF16)],
        compiler_params=_params("parallel", "arbitrary"),
        name="norm_matmul",
    )(x, g.reshape(1, K), w)


def _ffn_up_kernel(x_ref, g_ref, wg_ref, wu_ref, o_ref, xn_ref):
    def swiglu(xn, wg, wu):
        a = jnp.dot(xn, wg, preferred_element_type=F32)
        b = jnp.dot(xn, wu, preferred_element_type=F32)
        return (a * jax.nn.sigmoid(a) * b).astype(o_ref.dtype)

    @pl.when(pl.program_id(1) == 0)
    def _():
        wg, wu = wg_ref[...].astype(BF16), wu_ref[...].astype(BF16)
        for r0 in range(0, x_ref.shape[0], NORM_CHUNK):
            rows = pl.ds(r0, NORM_CHUNK)
            xn = _rms_normalize(x_ref[rows, :], g_ref[...]).astype(BF16)
            xn_ref[rows, :] = xn
            o_ref[rows, :] = swiglu(xn, wg, wu)

    @pl.when(pl.program_id(1) > 0)
    def _():
        o_ref[...] = swiglu(xn_ref[...], wg_ref[...].astype(BF16), wu_ref[...].astype(BF16))


def ffn_up(x, g, wg, wu, layer, *, tm=1024, tn=512):
    T, K = x.shape
    N = wg.shape[2]
    tm = _pick_tile(T, tm, 16)
    tn = _pick_tile(N, tn, V7X_LANES)
    return pl.pallas_call(
        _ffn_up_kernel,
        out_shape=jax.ShapeDtypeStruct((T, N), BF16),
        grid=(T // tm, N // tn),
        in_specs=[pl.BlockSpec((tm, K), lambda i, j: (i, 0)),
                  pl.BlockSpec((1, K), lambda i, j: (0, 0)),
                  pl.BlockSpec((None, K, tn), lambda i, j: (layer, 0, j)),
                  pl.BlockSpec((None, K, tn), lambda i, j: (layer, 0, j))],
        out_specs=pl.BlockSpec((tm, tn), lambda i, j: (i, j)),
        scratch_shapes=[pltpu.VMEM((tm, K), BF16)],
        compiler_params=_params("parallel", "arbitrary"),
        name="ffn_up",
    )(x, g.reshape(1, K), wg, wu)


def _matmul_res_kernel(x_ref, w_ref, r_ref, o_ref):
    o_ref[...] = r_ref[...] + jnp.dot(x_ref[...], w_ref[...], preferred_element_type=F32)


def matmul_res(x, w, layer, res, *, tm=1024, tn=512):
    T, K = x.shape
    N = w.shape[2]
    tm = _pick_tile(T, tm, 16)
    tn = _pick_tile(N, tn, V7X_LANES)
    return pl.pallas_call(
        _matmul_res_kernel,
        out_shape=jax.ShapeDtypeStruct((T, N), F32),
        grid=(T // tm, N // tn),
        in_specs=[pl.BlockSpec((tm, K), lambda i, j: (i, 0)),
                  pl.BlockSpec((None, K, tn), lambda i, j: (layer, 0, j)),
                  pl.BlockSpec((tm, tn), lambda i, j: (i, j))],
        out_specs=pl.BlockSpec((tm, tn), lambda i, j: (i, j)),
        compiler_params=_params("parallel", "parallel"),
        name="matmul_res",
    )(x, w, res)


def _matmul_res_norm_kernel(x_ref, w_ref, r_ref, g_ref, o_ref):
    k = pl.program_id(1)

    @pl.when(k == 0)
    def _():
        o_ref[...] = r_ref[...] + jnp.dot(x_ref[...], w_ref[...], preferred_element_type=F32)

    @pl.when(k > 0)
    def _():
        o_ref[...] += jnp.dot(x_ref[...], w_ref[...], preferred_element_type=F32)

    @pl.when(k == pl.num_programs(1) - 1)
    def _():
        o_ref[...] = _rms_normalize(o_ref[...], g_ref[...])


def matmul_res_norm(x, w, layer, res, g, *, tm=1024, tk=512):
    T, K = x.shape
    N = w.shape[2]
    tm = _pick_tile(T, tm, 16)
    tk = _pick_tile(K, tk, V7X_LANES)
    return pl.pallas_call(
        _matmul_res_norm_kernel,
        out_shape=jax.ShapeDtypeStruct((T, N), F32),
        grid=(T // tm, K // tk),
        in_specs=[pl.BlockSpec((tm, tk), lambda i, k: (i, k)),
                  pl.BlockSpec((None, tk, N), lambda i, k: (layer, k, 0)),
                  pl.BlockSpec((tm, N), lambda i, k: (i, 0)),
                  pl.BlockSpec((1, N), lambda i, k: (0, 0))],
        out_specs=pl.BlockSpec((tm, N), lambda i, k: (i, 0)),
        compiler_params=_params("parallel", "arbitrary"),
        name="matmul_res_norm",
    )(x, w, res, g.reshape(1, N))


def _merge_kernel(ya_ref, yb_ref, yc_ref, ga_ref, gb_ref, gc_ref, wa_ref, wb_ref, wc_ref, o_ref):
    def branch(y_ref, g_ref, w_ref):
        gate = jax.nn.sigmoid(g_ref[...].astype(F32))
        return gate * jnp.dot(y_ref[...], w_ref[...], preferred_element_type=F32)

    m = branch(ya_ref, ga_ref, wa_ref) + branch(yb_ref, gb_ref, wb_ref) + branch(yc_ref, gc_ref, wc_ref)
    o_ref[...] = m.astype(o_ref.dtype)


def merge_branches(ya, yb, yc, proj, gate_off, wa, wb, wc, layer, *, tm=1024, tn=1024):
    T, K = ya.shape
    N = wa.shape[2]
    tm = _pick_tile(T, tm, 16)
    tn = _pick_tile(N, tn, V7X_LANES)
    assert gate_off % tn == 0
    y_spec = pl.BlockSpec((tm, K), lambda i, j: (i, 0))
    w_spec = pl.BlockSpec((None, K, tn), lambda i, j: (layer, 0, j))

    def gate_spec(branch):
        base = (gate_off + branch * N) // tn
        return pl.BlockSpec((tm, tn), lambda i, j: (i, base + j))

    return pl.pallas_call(
        _merge_kernel,
        out_shape=jax.ShapeDtypeStruct((T, N), BF16),
        grid=(T // tm, N // tn),
        in_specs=[y_spec, y_spec, y_spec, gate_spec(0), gate_spec(1), gate_spec(2),
                  w_spec, w_spec, w_spec],
        out_specs=pl.BlockSpec((tm, tn), lambda i, j: (i, j)),
        compiler_params=_params("parallel", "parallel"),
        name="merge_branches",
    )(ya, yb, yc, proj, proj, proj, wa, wb, wc)


POOL_PAD = 8


def _pool_kernel(u_ref, halo_ref, w_ref, sc_ref, o_ref, ext_ref, stage_ref, *, tm, tiles_per_seq):
    t_in_seq = pl.program_id(0) % tiles_per_seq
    base = POOL_PAD + POOL_HALO
    rows = tm + POOL_HALO
    zeros = jnp.zeros((POOL_PAD, ext_ref.shape[1]), F32)
    ext_ref[0:POOL_PAD, :] = zeros
    ext_ref[POOL_PAD:base, :] = jnp.where(t_in_seq == 0, 0.0, halo_ref[...].astype(F32))
    ext_ref[base:, :] = u_ref[...].astype(F32)
    stage_ref[0:POOL_PAD, :] = zeros[:, 0:POOL_GROUP]
    pos = t_in_seq * tm + lax.broadcasted_iota(jnp.int32, (tm, 1), 0)
    for gi, win in enumerate(POOL_WINDOWS):
        cols = slice(gi * POOL_GROUP, (gi + 1) * POOL_GROUP)
        x = ext_ref[base:, cols]
        cur = ext_ref[pl.ds(POOL_PAD, rows), cols] + ext_ref[pl.ds(POOL_PAD - 1, rows), cols]
        span = 2
        while span < win:
            stage_ref[pl.ds(POOL_PAD, rows), :] = cur
            cur = cur + stage_ref[pl.ds(POOL_PAD - span, rows), :]
            span *= 2
        assert span == win <= 2 * POOL_PAD, "pooling windows must be powers of two up to 2 * POOL_PAD"
        total = cur[POOL_HALO:]
        cnt = jnp.minimum(pos + 1, win).astype(F32)
        pooled = total / cnt - x
        y = jnp.dot(pooled.astype(BF16), w_ref[gi], preferred_element_type=F32)
        o_ref[:, cols] = (y * sc_ref[:, cols]).astype(o_ref.dtype)


def pool_mixer(proj, u_off, w_grp, layer, scale, seq_len, *, tm=1024):
    T = proj.shape[0]
    width = len(POOL_WINDOWS) * POOL_GROUP
    tm = _pick_tile(seq_len, tm, POOL_HALO)
    assert u_off % width == 0 and max(POOL_WINDOWS) <= POOL_HALO
    ublk = u_off // width
    rows_per_tile = tm // POOL_HALO
    return pl.pallas_call(
        functools.partial(_pool_kernel, tm=tm, tiles_per_seq=seq_len // tm),
        out_shape=jax.ShapeDtypeStruct((T, width), BF16),
        grid=(T // tm,),
        in_specs=[pl.BlockSpec((tm, width), lambda i: (i, ublk)),
                  pl.BlockSpec((POOL_HALO, width),
                               lambda i: (jnp.maximum(i * rows_per_tile - 1, 0), ublk)),
                  pl.BlockSpec((None,) + w_grp.shape[1:], lambda i: (layer, 0, 0, 0)),
                  pl.BlockSpec((1, width), lambda i: (0, 0))],
        out_specs=pl.BlockSpec((tm, width), lambda i: (i, 0)),
        scratch_shapes=[pltpu.VMEM((POOL_PAD + POOL_HALO + tm, width), F32),
                        pltpu.VMEM((POOL_PAD + POOL_HALO + tm, POOL_GROUP), F32)],
        compiler_params=_params("parallel"),
        name="pool_mixer",
    )(proj, proj, w_grp, scale.reshape(1, width))


BF16_ROWS = 16


def _cast_plan(casts, grid):
    nsteps = grid[0] * grid[1] * grid[2]
    step = lambda h, b, i: (h * grid[1] + b) * grid[2] + i
    operands, in_specs, out_shapes, out_specs, shapes = [], [], [], [], []
    for w, layer in casts:
        depth, K, N = w.shape
        share = 1
        while (K * share) % (nsteps * BF16_ROWS) and share < nsteps:
            share *= 2
        assert (K * share) % (nsteps * BF16_ROWS) == 0 and nsteps % share == 0, (w.shape, nsteps)
        rows = K * share // nsteps
        operands.append(w)
        in_specs.append(pl.BlockSpec((None, rows, N), lambda h, b, i, layer=layer, share=share:
                                     (layer, step(h, b, i) // share, 0)))
        out_shapes.append(jax.ShapeDtypeStruct((K, N), BF16))
        out_specs.append(pl.BlockSpec((rows, N), lambda h, b, i, share=share: (step(h, b, i) // share, 0)))
        shapes.append((1, K, N))
    return operands, in_specs, out_shapes, out_specs, shapes


def _cast_slabs(src_refs, dst_refs):
    for src, dst in zip(src_refs, dst_refs):
        dst[...] = src[...].astype(BF16)


def _bucket_range(d_lo, d_hi):
    def bucket(d):
        if d < REL_MAX_EXACT:
            return d
        big = REL_MAX_EXACT + int(math.log(d / REL_MAX_EXACT) / math.log(REL_MAX_DIST / REL_MAX_EXACT)
                                  * (REL_BUCKETS - REL_MAX_EXACT))
        return min(big, REL_BUCKETS - 1)
    return max(bucket(max(d_lo, 0)) - 1, 0), min(bucket(d_hi) + 1, REL_BUCKETS - 1)


BIAS_ROWS = 8


def _bias_tiles_kernel(tab_ref, o_ref, *, heads, blk, ndist):
    key0 = lax.broadcasted_iota(jnp.int32, (BIAS_ROWS, blk), 0)
    qry = lax.broadcasted_iota(jnp.int32, (BIAS_ROWS, blk), 1)
    for dist in range(ndist):
        @pl.when(pl.program_id(0) == dist)
        def _(dist=dist):
            b_lo, b_hi = _bucket_range(dist * blk - (blk - 1), dist * blk + (blk - 1))
            table = [[tab_ref[h * REL_BUCKETS + b] * LOG2E for b in range(b_lo, b_hi + 1)]
                     for h in range(heads)]

            def rows(c, carry):
                r0 = pl.multiple_of(c * BIAS_ROWS, BIAS_ROWS)
                d = dist * blk + qry - (key0 + r0)
                n = jnp.maximum(d, 0)
                nf = jnp.maximum(n, 1).astype(F32)
                large = REL_MAX_EXACT + (jnp.log(nf / REL_MAX_EXACT)
                                         / math.log(REL_MAX_DIST / REL_MAX_EXACT)
                                         * (REL_BUCKETS - REL_MAX_EXACT)).astype(jnp.int32)
                large = jnp.minimum(large, REL_BUCKETS - 1)
                bucket = jnp.where(n < REL_MAX_EXACT, n, large)
                accs = [jnp.full((BIAS_ROWS, blk), table[h][-1], F32) for h in range(heads)]
                for i, b in enumerate(range(b_lo, b_hi)):
                    mask = bucket == b
                    accs = [jnp.where(mask, table[h][i], accs[h]) for h in range(heads)]
                for h in range(heads):
                    val = jnp.where(d >= 0, accs[h], NEG) if dist == 0 else accs[h]
                    o_ref[h, 0, pl.ds(r0, BIAS_ROWS), :] = val
                return carry

            lax.fori_loop(0, blk // BIAS_ROWS, rows, 0)


def rel_bias_tiles(rel_table, ndist, blk):
    heads = rel_table.shape[0]
    return pl.pallas_call(
        functools.partial(_bias_tiles_kernel, heads=heads, blk=blk, ndist=ndist),
        out_shape=jax.ShapeDtypeStruct((heads, ndist, blk, blk), F32),
        grid=(ndist,),
        in_specs=[pl.BlockSpec(memory_space=pltpu.SMEM)],
        out_specs=pl.BlockSpec((heads, 1, blk, blk), lambda d: (0, d, 0, 0)),
        compiler_params=_params("parallel"),
        name="rel_bias_tiles",
    )(rel_table.reshape(-1))


MOBA_VT_ROWS = HEAD_DIM + 8


def _moba_kernel(*refs, ncast, nb, blk, topk, scale):
    q_ref, k_ref, v_ref, bias_ref = refs[:4]
    o_ref = refs[4 + ncast]
    kaug_ref, vt_ref, kbar_ref, s_ref = refs[5 + 2 * ncast:]
    _cast_slabs(refs[4:4 + ncast], refs[5 + ncast:5 + 2 * ncast])
    qi = pl.program_id(2)
    nstream, tq, dh = q_ref.shape
    seq = k_ref.shape[1]
    nbp = kbar_ref.shape[1]

    @pl.when(qi == 0)
    def _():
        row_blk = lax.broadcasted_iota(jnp.int32, (seq, dh), 0) // blk
        col = lax.broadcasted_iota(jnp.int32, (seq, dh), 1)
        onehot = (row_blk == col).astype(BF16)
        ones_rows = (lax.broadcasted_iota(jnp.int32, (MOBA_VT_ROWS - dh, tq), 0) == 0).astype(BF16)
        for r in range(nstream):
            kaug_ref[r, :, 0:dh] = (k_ref[r].astype(F32) * (scale * LOG2E)).astype(BF16)
            kaug_ref[r, :, dh:] = onehot
            for t in range(seq // tq):
                vt_ref[r, t, 0:dh, :] = v_ref[r, t * tq:(t + 1) * tq, :].astype(F32).T.astype(BF16)
                vt_ref[r, t, dh:, :] = ones_rows
            kbar_ref[r] = jnp.zeros(kbar_ref.shape[1:], F32)
            for n in range(nb):
                kblk = k_ref[r, n * blk:(n + 1) * blk, :].astype(F32)
                kbar_ref[r, n:n + 1, :] = jnp.mean(kblk, axis=0, keepdims=True)

    blk_id = lax.broadcasted_iota(jnp.int32, (nbp, tq), 0)
    own = qi * (tq // blk) + lax.broadcasted_iota(jnp.int32, (nbp, tq), 1) // blk
    past = blk_id < own

    def augmented_query(r):
        q = q_ref[r]
        kbar = kbar_ref[r]
        kbar_hi = kbar.astype(BF16)
        kbar_lo = (kbar - kbar_hi.astype(F32)).astype(BF16)
        gate = (lax.dot_general(kbar_hi, q, _NT, preferred_element_type=F32)
                + lax.dot_general(kbar_lo, q, _NT, preferred_element_type=F32))
        gate = jnp.where(past, gate, -jnp.inf)
        rank = jnp.zeros((nbp, tq), jnp.int32)
        for m in range(nb - 1):
            gm = gate[m:m + 1, :]
            before = (gm > gate) | ((gm == gate) & (m < blk_id))
            rank = rank + before.astype(jnp.int32)
        chosen = (past & (rank < topk)) | (blk_id == own)
        pen_t = jnp.where(chosen, 0.0, NEG)
        pen_t = jnp.concatenate([pen_t, jnp.zeros((dh - nbp, tq), F32)], axis=0)
        return jnp.concatenate([q, pen_t.T.astype(BF16)], axis=1)

    q_aug = [augmented_query(r) for r in range(nstream)]

    def produce(r, dist):
        start = pl.multiple_of((qi - dist) * tq, tq)
        s = lax.dot_general(kaug_ref[r, pl.ds(start, tq), :], q_aug[r], _NT,
                            preferred_element_type=F32) + bias_ref[dist]
        s_ref[dist % 2, r] = s
        return jnp.max(s, axis=0, keepdims=True)

    def absorb(r, dist, s_max, m_i, acc):
        m_new = jnp.maximum(m_i, s_max)
        p = jnp.exp2(s_ref[dist % 2, r] - m_new)
        acc = (jnp.exp2(m_i - m_new) * acc
               + jnp.dot(vt_ref[r, qi - dist], p.astype(BF16), preferred_element_type=F32))
        return m_new, acc

    def body(dist, carry):
        done = [absorb(r, dist, *carry[r]) for r in range(nstream)]
        return tuple((produce(r, dist + 1),) + done[r] for r in range(nstream))

    init = tuple((produce(r, 0), jnp.full((1, tq), NEG, F32), jnp.zeros((MOBA_VT_ROWS, tq), F32))
                 for r in range(nstream))
    last = lax.fori_loop(0, qi, body, init)
    for r in range(nstream):
        _, acc = absorb(r, qi, *last[r])
        o_ref[r] = (acc[0:dh] / acc[dh:dh + 1]).T.astype(o_ref.dtype)


def moba_attention(proj3, q_off, k_off, v_off, bias_tiles, heads, *, nstream=4, casts=()):
    B, S, _ = proj3.shape
    blk, dh = MOBA_BLOCK, HEAD_DIM
    tq = bias_tiles.shape[-1]
    assert S % tq == 0 and tq % blk == 0 and q_off % dh == 0 and k_off % dh == 0 and v_off % dh == 0
    nstream = nstream if B % nstream == 0 else 1
    nb = S // blk
    nbp = -(-nb // 8) * 8
    assert nbp <= dh, "block one-hot columns must fit beside the keys"
    topk = min(MOBA_TOPK, nb)
    qb, kb_, vb_ = q_off // dh, k_off // dh, v_off // dh
    grid = (heads, B // nstream, S // tq)
    c_ops, c_in, c_shapes, c_out, c_views = _cast_plan(casts, grid)
    kern = functools.partial(_moba_kernel, ncast=len(casts), nb=nb, blk=blk, topk=topk, scale=dh ** -0.5)
    out, *cast = pl.pallas_call(
        kern,
        out_shape=[jax.ShapeDtypeStruct((B, S, heads * dh), BF16)] + c_shapes,
        grid=grid,
        in_specs=[pl.BlockSpec((nstream, tq, dh), lambda h, b, i: (b, i, qb + h)),
                  pl.BlockSpec((nstream, S, dh), lambda h, b, i: (b, 0, kb_ + h),
                               pipeline_mode=pl.Buffered(1)),
                  pl.BlockSpec((nstream, S, dh), lambda h, b, i: (b, 0, vb_ + h),
                               pipeline_mode=pl.Buffered(1)),
                  pl.BlockSpec((None, S // tq, tq, tq), lambda h, b, i: (h, 0, 0, 0),
                               pipeline_mode=pl.Buffered(1))] + c_in,
        out_specs=[pl.BlockSpec((nstream, tq, dh), lambda h, b, i: (b, i, h))] + c_out,
        scratch_shapes=[pltpu.VMEM((nstream, S, 2 * dh), BF16),
                        pltpu.VMEM((nstream, S // tq, MOBA_VT_ROWS, tq), BF16),
                        pltpu.VMEM((nstream, nbp, dh), F32),
                        pltpu.VMEM((2, nstream, tq, tq), F32)],
        compiler_params=_params("parallel", "parallel", "arbitrary"),
        name="moba_attention",
    )(proj3, proj3, proj3, bias_tiles, *c_ops)
    return out, [c.reshape(v) for c, v in zip(cast, c_views)]


def _suffix_sum_matrix(bk):
    r = lax.broadcasted_iota(jnp.int32, (2 * bk, 2 * bk), 0) % bk
    c = lax.broadcasted_iota(jnp.int32, (2 * bk, 2 * bk), 1)
    return ((c >= bk) | (r > c)).astype(BF16)


def _sb_kernel(*refs, ncast, scale):
    q_ref, k_ref, v_ref = refs[:3]
    o_ref = refs[3 + ncast]
    _cast_slabs(refs[3:3 + ncast], refs[4 + ncast:4 + 2 * ncast])
    qi = pl.program_id(2)
    nbatch, tq, width = q_ref.shape
    dh = HEAD_DIM
    half = tq // 2
    sums = _suffix_sum_matrix(half)
    streams = [(r, c0) for r in range(nbatch) for c0 in range(0, width, dh)]
    nstream = len(streams)
    q2 = [(q_ref[r, :, c0:c0 + dh].astype(F32) * (scale * LOG2E)).astype(BF16) for r, c0 in streams]

    def tile(s, j, run, acc, valid):
        r, c0 = streams[s]
        start = pl.multiple_of(j * tq, tq)
        z = lax.dot_general(q2[s], k_ref[r, pl.ds(start, tq), c0:c0 + dh], _NT,
                            preferred_element_type=F32)
        sp = jnp.maximum(z, 0.0) + jnp.log2(1.0 + jnp.exp2(-jnp.abs(z)))
        log_1m = -sp
        if valid is not None:
            log_1m = jnp.where(valid, log_1m, 0.0)
        hi = log_1m.astype(BF16)
        lo = (log_1m - hi.astype(F32)).astype(BF16)
        cs_b = jnp.dot(jnp.concatenate([hi[:, half:], lo[:, half:]], axis=1), sums,
                       preferred_element_type=F32)
        cs_a = jnp.dot(jnp.concatenate([hi[:, :half], lo[:, :half]], axis=1), sums,
                       preferred_element_type=F32)
        run_b = run + cs_b[:, half:]
        after = jnp.concatenate([run_b + cs_a[:, :half], run + cs_b[:, :half]], axis=1)
        a = jnp.exp2((z - sp) + after)
        if valid is not None:
            a = jnp.where(valid, a, 0.0)
        acc = acc + jnp.dot(a.astype(BF16), v_ref[r, pl.ds(start, tq), c0:c0 + dh],
                            preferred_element_type=F32)
        return run_b + cs_a[:, half:], acc

    row = lax.broadcasted_iota(jnp.int32, (tq, tq), 0)
    col = lax.broadcasted_iota(jnp.int32, (tq, tq), 1)
    no_prev = jnp.where(qi > 0, 0.0, 4.0 * EXP2_ZERO_BELOW)
    state = []
    for s in range(nstream):
        run, acc = tile(s, qi, jnp.zeros((tq, half), F32), jnp.zeros((tq, dh), F32), col < row)
        state.append(tile(s, jnp.maximum(qi - 1, 0), run + no_prev, acc, None))

    def run_max(state):
        return functools.reduce(jnp.maximum, [jnp.max(run) for run, _ in state])

    def cond(carry):
        j, _, top = carry
        return (j >= 0) & (top > EXP2_ZERO_BELOW)

    def body(carry):
        j, state, _ = carry
        state = tuple(tile(s, j, run, acc, None) for s, (run, acc) in enumerate(state))
        return j - 1, state, run_max(state)

    _, state, _ = lax.while_loop(cond, body, (qi - 2, tuple(state), run_max(state)))
    for (r, c0), (_, acc) in zip(streams, state):
        o_ref[r, :, c0:c0 + dh] = acc.astype(o_ref.dtype)


def stick_breaking_attention(proj3, q_off, k_off, v_off, heads, *, tq=2 * V7X_LANES, nbatch=4, nhead=2,
                             casts=()):
    B, S, _ = proj3.shape
    dh = HEAD_DIM
    assert tq == 2 * V7X_LANES, "suffix sums work on 128-key halves of a tile"
    nbatch = nbatch if B % nbatch == 0 else 1
    width = nhead * dh
    assert S % tq == 0 and heads % nhead == 0
    assert q_off % width == 0 and k_off % width == 0 and v_off % width == 0
    qb, kb_, vb_ = q_off // width, k_off // width, v_off // width
    grid = (heads // nhead, B // nbatch, S // tq)
    c_ops, c_in, c_shapes, c_out, c_views = _cast_plan(casts, grid)
    out, *cast = pl.pallas_call(
        functools.partial(_sb_kernel, ncast=len(casts), scale=dh ** -0.5),
        out_shape=[jax.ShapeDtypeStruct((B, S, heads * dh), BF16)] + c_shapes,
        grid=grid,
        in_specs=[pl.BlockSpec((nbatch, tq, width), lambda h, b, i: (b, i, qb + h)),
                  pl.BlockSpec((nbatch, S, width), lambda h, b, i: (b, 0, kb_ + h)),
                  pl.BlockSpec((nbatch, S, width), lambda h, b, i: (b, 0, vb_ + h))] + c_in,
        out_specs=[pl.BlockSpec((nbatch, tq, width), lambda h, b, i: (b, i, h))] + c_out,
        compiler_params=_params("parallel", "parallel", "parallel"),
        name="stick_breaking_attention",
    )(proj3, proj3, proj3, *c_ops)
    return out, [c.reshape(v) for c, v in zip(cast, c_views)]


def kernel(x, norm_mix, norm_ffn, w_in, w_pool, pool_scale, w_br_a, w_br_b, w_br_c, w_out, w_gate,
           w_up, w_down, rel_bias, norm_final):
    B, S, D = x.shape
    depth = w_in.shape[0]
    T = B * S
    moba_w = MOBA_HEADS * HEAD_DIM
    sb_w = SB_HEADS * HEAD_DIM
    pool_w = len(POOL_WINDOWS) * POOL_GROUP
    off_qa, off_ka, off_va = 0, moba_w, 2 * moba_w
    off_u = 3 * moba_w
    off_qs = off_u + pool_w
    off_ks, off_vs = off_qs + sb_w, off_qs + 2 * sb_w
    off_gates = off_qs + 3 * sb_w

    bias_tiles = rel_bias_tiles(rel_bias, S // MOBA_QTILE, MOBA_QTILE)
    w_pool = w_pool.astype(BF16)
    h = x.reshape(T, D)
    w_in_l, layer_in = w_in, 0
    for l in range(depth):
        proj = norm_matmul(h, norm_mix[l], w_in_l, layer_in)
        proj3 = proj.reshape(B, S, proj.shape[1])
        moba_casts = [(w_down, l)] + ([(w_in, l + 1)] if l + 1 < depth else [])
        ya, moba_cast = moba_attention(proj3, off_qa, off_ka, off_va, bias_tiles, MOBA_HEADS,
                                       casts=moba_casts)
        yb = pool_mixer(proj, off_u, w_pool, l, pool_scale[l], S)
        sb_casts = [(w, l) for w in (w_gate, w_up, w_br_a, w_br_b, w_br_c, w_out)]
        yc, (w_gate_l, w_up_l, wa_l, wb_l, wc_l, w_out_l) = stick_breaking_attention(
            proj3, off_qs, off_ks, off_vs, SB_HEADS, casts=sb_casts)
        m = merge_branches(ya.reshape(T, moba_w), yb, yc.reshape(T, sb_w), proj, off_gates,
                           wa_l, wb_l, wc_l, 0)
        h = matmul_res(m, w_out_l, 0, h, tn=1024)
        act = ffn_up(h, norm_ffn[l], w_gate_l, w_up_l, 0)
        if l + 1 < depth:
            h = matmul_res(act, moba_cast[0], 0, h)
            w_in_l, layer_in = moba_cast[1], 0
        else:
            h = matmul_res_norm(act, moba_cast[0], 0, h, norm_final)
    return h.reshape(B, S, D)
```

```python
import functools
import math

import jax
import jax.numpy as jnp
from jax import lax
from jax.experimental import pallas as pl
from jax.experimental.pallas import tpu as pltpu

F32 = jnp.float32
BF16 = jnp.bfloat16

HEAD_DIM = 128
MOBA_HEADS = 8
MOBA_BLOCK = 256
MOBA_TOPK = 3
MOBA_QTILE = 512
SB_HEADS = 8
POOL_WINDOWS = (2, 4, 8, 16)
POOL_GROUP = 256
POOL_HALO = 16
REL_BUCKETS = 32
REL_MAX_EXACT = 16
REL_MAX_DIST = 2048
EPS = 1e-6
NEG = -1e30
LOG2E = math.log2(math.e)
EXP2_ZERO_BELOW = -152.0

V7X_LANES = 128
V7X_VMEM_LIMIT_BYTES = 56 * 1024 * 1024

_NT = (((1,), (1,)), ((), ()))


def _params(*semantics):
    return pltpu.CompilerParams(dimension_semantics=semantics,
                                vmem_limit_bytes=V7X_VMEM_LIMIT_BYTES)


def _pick_tile(n, want, align):
    t = min(want, n)
    while t > align and (n % t or t % align):
        t -= align
    assert n % t == 0 and t % align == 0, (n, want, align)
    return t


def _rms_normalize(x, g):
    ms = jnp.mean(x * x, axis=-1, keepdims=True)
    return x * lax.rsqrt(ms + EPS) * g


NORM_CHUNK = 256


def _norm_matmul_kernel(x_ref, g_ref, w_ref, o_ref, xn_ref):
    @pl.when(pl.program_id(1) == 0)
    def _():
        w = w_ref[...].astype(BF16)
        for r0 in range(0, x_ref.shape[0], NORM_CHUNK):
            rows = pl.ds(r0, NORM_CHUNK)
            xn = _rms_normalize(x_ref[rows, :], g_ref[...]).astype(BF16)
            xn_ref[rows, :] = xn
            o_ref[rows, :] = jnp.dot(xn, w, preferred_element_type=F32).astype(o_ref.dtype)

    @pl.when(pl.program_id(1) > 0)
    def _():
        o_ref[...] = jnp.dot(xn_ref[...], w_ref[...].astype(BF16),
                             preferred_element_type=F32).astype(o_ref.dtype)


def norm_matmul(x, g, w, layer, *, out_dtype=BF16, tm=1024, tn=1024):
    T, K = x.shape
    N = w.shape[2]
    tm = _pick_tile(T, tm, 16)
    tn = _pick_tile(N, tn, V7X_LANES)
    return pl.pallas_call(
        _norm_matmul_kernel,
        out_shape=jax.ShapeDtypeStruct((T, N), out_dtype),
        grid=(T // tm, N // tn),
        in_specs=[pl.BlockSpec((tm, K), lambda i, j: (i, 0)),
                  pl.BlockSpec((1, K), lambda i, j: (0, 0)),
                  pl.BlockSpec((None, K, tn), lambda i, j: (layer, 0, j))],
        out_specs=pl.BlockSpec((tm, tn), lambda i, j: (i, j)),
        scratch_shapes=[pltpu.VMEM((tm, K), BF16)],
        compiler_params=_params("parallel", "arbitrary"),
        name="norm_matmul",
    )(x, g.reshape(1, K), w)


def _ffn_up_kernel(x_ref, g_ref, wg_ref, wu_ref, o_ref, xn_ref):
    def swiglu(xn, wg, wu):
        a = jnp.dot(xn, wg, preferred_element_type=F32)
        b = jnp.dot(xn, wu, preferred_element_type=F32)
        return (a * jax.nn.sigmoid(a) * b).astype(o_ref.dtype)

    @pl.when(pl.program_id(1) == 0)
    def _():
        wg, wu = wg_ref[...].astype(BF16), wu_ref[...].astype(BF16)
        for r0 in range(0, x_ref.shape[0], NORM_CHUNK):
            rows = pl.ds(r0, NORM_CHUNK)
            xn = _rms_normalize(x_ref[rows, :], g_ref[...]).astype(BF16)
            xn_ref[rows, :] = xn
            o_ref[rows, :] = swiglu(xn, wg, wu)

    @pl.when(pl.program_id(1) > 0)
    def _():
        o_ref[...] = swiglu(xn_ref[...], wg_ref[...].astype(BF16), wu_ref[...].astype(BF16))


def ffn_up(x, g, wg, wu, layer, *, tm=1024, tn=512):
    T, K = x.shape
    N = wg.shape[2]
    tm = _pick_tile(T, tm, 16)
    tn = _pick_tile(N, tn, V7X_LANES)
    return pl.pallas_call(
        _ffn_up_kernel,
        out_shape=jax.ShapeDtypeStruct((T, N), BF16),
        grid=(T // tm, N // tn),
        in_specs=[pl.BlockSpec((tm, K), lambda i, j: (i, 0)),
                  pl.BlockSpec((1, K), lambda i, j: (0, 0)),
                  pl.BlockSpec((None, K, tn), lambda i, j: (layer, 0, j)),
                  pl.BlockSpec((None, K, tn), lambda i, j: (layer, 0, j))],
        out_specs=pl.BlockSpec((tm, tn), lambda i, j: (i, j)),
        scratch_shapes=[pltpu.VMEM((tm, K), BF16)],
        compiler_params=_params("parallel", "arbitrary"),
        name="ffn_up",
    )(x, g.reshape(1, K), wg, wu)


def _matmul_res_kernel(x_ref, w_ref, r_ref, o_ref):
    o_ref[...] = r_ref[...] + jnp.dot(x_ref[...], w_ref[...], preferred_element_type=F32)


def matmul_res(x, w, layer, res, *, tm=1024, tn=512):
    T, K = x.shape
    N = w.shape[2]
    tm = _pick_tile(T, tm, 16)
    tn = _pick_tile(N, tn, V7X_LANES)
    return pl.pallas_call(
        _matmul_res_kernel,
        out_shape=jax.ShapeDtypeStruct((T, N), F32),
        grid=(T // tm, N // tn),
        in_specs=[pl.BlockSpec((tm, K), lambda i, j: (i, 0)),
                  pl.BlockSpec((None, K, tn), lambda i, j: (layer, 0, j)),
                  pl.BlockSpec((tm, tn), lambda i, j: (i, j))],
        out_specs=pl.BlockSpec((tm, tn), lambda i, j: (i, j)),
        compiler_params=_params("parallel", "parallel"),
        name="matmul_res",
    )(x, w, res)


def _matmul_res_norm_kernel(x_ref, w_ref, r_ref, g_ref, o_ref):
    k = pl.program_id(1)
    last = pl.num_programs(1) - 1

    @pl.when(k == 0)
    def _():
        o_ref[...] = r_ref[...] + jnp.dot(x_ref[...], w_ref[...], preferred_element_type=F32)

    @pl.when((k > 0) & (k < last))
    def _():
        o_ref[...] += jnp.dot(x_ref[...], w_ref[...], preferred_element_type=F32)

    @pl.when(k == last)
    def _():
        w = w_ref[...]
        for r0 in range(0, x_ref.shape[0], NORM_CHUNK):
            rows = pl.ds(r0, NORM_CHUNK)
            h = o_ref[rows, :] + jnp.dot(x_ref[rows, :], w, preferred_element_type=F32)
            o_ref[rows, :] = _rms_normalize(h, g_ref[...])


def matmul_res_norm(x, w, layer, res, g, *, tm=1024, tk=512):
    T, K = x.shape
    N = w.shape[2]
    tm = _pick_tile(T, tm, 16)
    tk = _pick_tile(K, tk, V7X_LANES)
    assert K // tk >= 2 and tm % NORM_CHUNK == 0, "first and last contraction steps must differ"
    return pl.pallas_call(
        _matmul_res_norm_kernel,
        out_shape=jax.ShapeDtypeStruct((T, N), F32),
        grid=(T // tm, K // tk),
        in_specs=[pl.BlockSpec((tm, tk), lambda i, k: (i, k)),
                  pl.BlockSpec((None, tk, N), lambda i, k: (layer, k, 0)),
                  pl.BlockSpec((tm, N), lambda i, k: (i, 0)),
                  pl.BlockSpec((1, N), lambda i, k: (0, 0))],
        out_specs=pl.BlockSpec((tm, N), lambda i, k: (i, 0)),
        compiler_params=_params("parallel", "arbitrary"),
        name="matmul_res_norm",
    )(x, w, res, g.reshape(1, N))


def _merge_kernel(ya_ref, yb_ref, yc_ref, ga_ref, gb_ref, gc_ref, wa_ref, wb_ref, wc_ref, o_ref):
    def branch(y_ref, g_ref, w_ref):
        gate = jax.nn.sigmoid(g_ref[...].astype(F32))
        return gate * jnp.dot(y_ref[...], w_ref[...], preferred_element_type=F32)

    m = branch(ya_ref, ga_ref, wa_ref) + branch(yb_ref, gb_ref, wb_ref) + branch(yc_ref, gc_ref, wc_ref)
    o_ref[...] = m.astype(o_ref.dtype)


def merge_branches(ya, yb, yc, proj, gate_off, wa, wb, wc, layer, *, tm=1024, tn=1024):
    T, K = ya.shape
    N = wa.shape[2]
    tm = _pick_tile(T, tm, 16)
    tn = _pick_tile(N, tn, V7X_LANES)
    assert gate_off % tn == 0
    y_spec = pl.BlockSpec((tm, K), lambda i, j: (i, 0))
    w_spec = pl.BlockSpec((None, K, tn), lambda i, j: (layer, 0, j))

    def gate_spec(branch):
        base = (gate_off + branch * N) // tn
        return pl.BlockSpec((tm, tn), lambda i, j: (i, base + j))

    return pl.pallas_call(
        _merge_kernel,
        out_shape=jax.ShapeDtypeStruct((T, N), BF16),
        grid=(T // tm, N // tn),
        in_specs=[y_spec, y_spec, y_spec, gate_spec(0), gate_spec(1), gate_spec(2),
                  w_spec, w_spec, w_spec],
        out_specs=pl.BlockSpec((tm, tn), lambda i, j: (i, j)),
        compiler_params=_params("parallel", "parallel"),
        name="merge_branches",
    )(ya, yb, yc, proj, proj, proj, wa, wb, wc)


POOL_PAD = 8


def _pool_kernel(u_ref, halo_ref, w_ref, sc_ref, o_ref, ext_ref, stage_ref, *, tm, tiles_per_seq):
    t_in_seq = pl.program_id(0) % tiles_per_seq
    base = POOL_PAD + POOL_HALO
    rows = tm + POOL_HALO
    zeros = jnp.zeros((POOL_PAD, ext_ref.shape[1]), F32)
    ext_ref[0:POOL_PAD, :] = zeros
    ext_ref[POOL_PAD:base, :] = jnp.where(t_in_seq == 0, 0.0, halo_ref[...].astype(F32))
    ext_ref[base:, :] = u_ref[...].astype(F32)
    stage_ref[0:POOL_PAD, :] = zeros[:, 0:POOL_GROUP]
    pos = t_in_seq * tm + lax.broadcasted_iota(jnp.int32, (tm, 1), 0)
    for gi, win in enumerate(POOL_WINDOWS):
        cols = slice(gi * POOL_GROUP, (gi + 1) * POOL_GROUP)
        x = ext_ref[base:, cols]
        cur = ext_ref[pl.ds(POOL_PAD, rows), cols] + ext_ref[pl.ds(POOL_PAD - 1, rows), cols]
        span = 2
        while span < win:
            stage_ref[pl.ds(POOL_PAD, rows), :] = cur
            cur = cur + stage_ref[pl.ds(POOL_PAD - span, rows), :]
            span *= 2
        assert span == win <= 2 * POOL_PAD, "pooling windows must be powers of two up to 2 * POOL_PAD"
        total = cur[POOL_HALO:]
        cnt = jnp.minimum(pos + 1, win).astype(F32)
        pooled = total / cnt - x
        y = jnp.dot(pooled.astype(BF16), w_ref[gi], preferred_element_type=F32)
        o_ref[:, cols] = (y * sc_ref[:, cols]).astype(o_ref.dtype)


def pool_mixer(proj, u_off, w_grp, layer, scale, seq_len, *, tm=1024):
    T = proj.shape[0]
    width = len(POOL_WINDOWS) * POOL_GROUP
    tm = _pick_tile(seq_len, tm, POOL_HALO)
    assert u_off % width == 0 and max(POOL_WINDOWS) <= POOL_HALO
    ublk = u_off // width
    rows_per_tile = tm // POOL_HALO
    return pl.pallas_call(
        functools.partial(_pool_kernel, tm=tm, tiles_per_seq=seq_len // tm),
        out_shape=jax.ShapeDtypeStruct((T, width), BF16),
        grid=(T // tm,),
        in_specs=[pl.BlockSpec((tm, width), lambda i: (i, ublk)),
                  pl.BlockSpec((POOL_HALO, width),
                               lambda i: (jnp.maximum(i * rows_per_tile - 1, 0), ublk)),
                  pl.BlockSpec((None,) + w_grp.shape[1:], lambda i: (layer, 0, 0, 0)),
                  pl.BlockSpec((1, width), lambda i: (0, 0))],
        out_specs=pl.BlockSpec((tm, width), lambda i: (i, 0)),
        scratch_shapes=[pltpu.VMEM((POOL_PAD + POOL_HALO + tm, width), F32),
                        pltpu.VMEM((POOL_PAD + POOL_HALO + tm, POOL_GROUP), F32)],
        compiler_params=_params("parallel"),
        name="pool_mixer",
    )(proj, proj, w_grp, scale.reshape(1, width))


BF16_ROWS = 16


def _cast_plan(casts, grid):
    nsteps = grid[0] * grid[1] * grid[2]
    step = lambda h, b, i: (h * grid[1] + b) * grid[2] + i
    operands, in_specs, out_shapes, out_specs, shapes = [], [], [], [], []
    for w, layer in casts:
        depth, K, N = w.shape
        share = 1
        while (K * share) % (nsteps * BF16_ROWS) and share < nsteps:
            share *= 2
        assert (K * share) % (nsteps * BF16_ROWS) == 0 and nsteps % share == 0, (w.shape, nsteps)
        rows = K * share // nsteps
        operands.append(w)
        in_specs.append(pl.BlockSpec((None, rows, N), lambda h, b, i, layer=layer, share=share:
                                     (layer, step(h, b, i) // share, 0)))
        out_shapes.append(jax.ShapeDtypeStruct((K, N), BF16))
        out_specs.append(pl.BlockSpec((rows, N), lambda h, b, i, share=share: (step(h, b, i) // share, 0)))
        shapes.append((1, K, N))
    return operands, in_specs, out_shapes, out_specs, shapes


def _cast_slabs(src_refs, dst_refs):
    for src, dst in zip(src_refs, dst_refs):
        dst[...] = src[...].astype(BF16)


def _bucket_range(d_lo, d_hi):
    def bucket(d):
        if d < REL_MAX_EXACT:
            return d
        big = REL_MAX_EXACT + int(math.log(d / REL_MAX_EXACT) / math.log(REL_MAX_DIST / REL_MAX_EXACT)
                                  * (REL_BUCKETS - REL_MAX_EXACT))
        return min(big, REL_BUCKETS - 1)
    return max(bucket(max(d_lo, 0)) - 1, 0), min(bucket(d_hi) + 1, REL_BUCKETS - 1)


BIAS_ROWS = 8


def _bias_tiles_kernel(tab_ref, o_ref, *, heads, blk, ndist):
    key0 = lax.broadcasted_iota(jnp.int32, (BIAS_ROWS, blk), 0)
    qry = lax.broadcasted_iota(jnp.int32, (BIAS_ROWS, blk), 1)
    for dist in range(ndist):
        @pl.when(pl.program_id(0) == dist)
        def _(dist=dist):
            b_lo, b_hi = _bucket_range(dist * blk - (blk - 1), dist * blk + (blk - 1))
            table = [[tab_ref[h * REL_BUCKETS + b] * LOG2E for b in range(b_lo, b_hi + 1)]
                     for h in range(heads)]

            def rows(c, carry):
                r0 = pl.multiple_of(c * BIAS_ROWS, BIAS_ROWS)
                d = dist * blk + qry - (key0 + r0)
                n = jnp.maximum(d, 0)
                nf = jnp.maximum(n, 1).astype(F32)
                large = REL_MAX_EXACT + (jnp.log(nf / REL_MAX_EXACT)
                                         / math.log(REL_MAX_DIST / REL_MAX_EXACT)
                                         * (REL_BUCKETS - REL_MAX_EXACT)).astype(jnp.int32)
                large = jnp.minimum(large, REL_BUCKETS - 1)
                bucket = jnp.where(n < REL_MAX_EXACT, n, large)
                accs = [jnp.full((BIAS_ROWS, blk), table[h][-1], F32) for h in range(heads)]
                for i, b in enumerate(range(b_lo, b_hi)):
                    mask = bucket == b
                    accs = [jnp.where(mask, table[h][i], accs[h]) for h in range(heads)]
                for h in range(heads):
                    val = jnp.where(d >= 0, accs[h], NEG) if dist == 0 else accs[h]
                    o_ref[h, 0, pl.ds(r0, BIAS_ROWS), :] = val
                return carry

            lax.fori_loop(0, blk // BIAS_ROWS, rows, 0)


def rel_bias_tiles(rel_table, ndist, blk):
    heads = rel_table.shape[0]
    return pl.pallas_call(
        functools.partial(_bias_tiles_kernel, heads=heads, blk=blk, ndist=ndist),
        out_shape=jax.ShapeDtypeStruct((heads, ndist, blk, blk), F32),
        grid=(ndist,),
        in_specs=[pl.BlockSpec(memory_space=pltpu.SMEM)],
        out_specs=pl.BlockSpec((heads, 1, blk, blk), lambda d: (0, d, 0, 0)),
        compiler_params=_params("parallel"),
        name="rel_bias_tiles",
    )(rel_table.reshape(-1))


MOBA_VT_ROWS = HEAD_DIM + 8


def _moba_kernel(*refs, ncast, nb, blk, topk, scale):
    q_ref, k_ref, v_ref, bias_ref = refs[:4]
    o_ref = refs[4 + ncast]
    kaug_ref, vt_ref, kbar_ref, s_ref = refs[5 + 2 * ncast:]
    _cast_slabs(refs[4:4 + ncast], refs[5 + ncast:5 + 2 * ncast])
    qi = pl.program_id(2)
    nstream, tq, dh = q_ref.shape
    seq = k_ref.shape[1]
    nbp = kbar_ref.shape[1]

    @pl.when(qi == 0)
    def _():
        row_blk = lax.broadcasted_iota(jnp.int32, (seq, dh), 0) // blk
        col = lax.broadcasted_iota(jnp.int32, (seq, dh), 1)
        onehot = (row_blk == col).astype(BF16)
        ones_rows = (lax.broadcasted_iota(jnp.int32, (MOBA_VT_ROWS - dh, tq), 0) == 0).astype(BF16)
        for r in range(nstream):
            kaug_ref[r, :, 0:dh] = (k_ref[r].astype(F32) * (scale * LOG2E)).astype(BF16)
            kaug_ref[r, :, dh:] = onehot
            for t in range(seq // tq):
                vt_ref[r, t, 0:dh, :] = v_ref[r, t * tq:(t + 1) * tq, :].astype(F32).T.astype(BF16)
                vt_ref[r, t, dh:, :] = ones_rows
            kbar_ref[r] = jnp.zeros(kbar_ref.shape[1:], F32)
            for n in range(nb):
                kblk = k_ref[r, n * blk:(n + 1) * blk, :].astype(F32)
                kbar_ref[r, n:n + 1, :] = jnp.mean(kblk, axis=0, keepdims=True)

    blk_id = lax.broadcasted_iota(jnp.int32, (nbp, tq), 0)
    own = qi * (tq // blk) + lax.broadcasted_iota(jnp.int32, (nbp, tq), 1) // blk
    past = blk_id < own

    def augmented_query(r):
        q = q_ref[r]
        kbar = kbar_ref[r]
        kbar_hi = kbar.astype(BF16)
        kbar_lo = (kbar - kbar_hi.astype(F32)).astype(BF16)
        gate = (lax.dot_general(kbar_hi, q, _NT, preferred_element_type=F32)
                + lax.dot_general(kbar_lo, q, _NT, preferred_element_type=F32))
        gate = jnp.where(past, gate, -jnp.inf)
        rank = jnp.zeros((nbp, tq), jnp.int32)
        for m in range(nb - 1):
            gm = gate[m:m + 1, :]
            before = (gm > gate) | ((gm == gate) & (m < blk_id))
            rank = rank + before.astype(jnp.int32)
        chosen = (past & (rank < topk)) | (blk_id == own)
        pen_t = jnp.where(chosen, 0.0, NEG)
        pen_t = jnp.concatenate([pen_t, jnp.zeros((dh - nbp, tq), F32)], axis=0)
        return jnp.concatenate([q, pen_t.T.astype(BF16)], axis=1)

    q_aug = [augmented_query(r) for r in range(nstream)]

    def produce(r, dist):
        start = pl.multiple_of((qi - dist) * tq, tq)
        s = lax.dot_general(kaug_ref[r, pl.ds(start, tq), :], q_aug[r], _NT,
                            preferred_element_type=F32) + bias_ref[dist]
        s_ref[dist % 2, r] = s
        return jnp.max(s, axis=0, keepdims=True)

    def absorb(r, dist, s_max, m_i, acc):
        m_new = jnp.maximum(m_i, s_max)
        p = jnp.exp2(s_ref[dist % 2, r] - m_new)
        acc = (jnp.exp2(m_i - m_new) * acc
               + jnp.dot(vt_ref[r, qi - dist], p.astype(BF16), preferred_element_type=F32))
        return m_new, acc

    def body(dist, carry):
        done = [absorb(r, dist, *carry[r]) for r in range(nstream)]
        return tuple((produce(r, dist + 1),) + done[r] for r in range(nstream))

    init = tuple((produce(r, 0), jnp.full((1, tq), NEG, F32), jnp.zeros((MOBA_VT_ROWS, tq), F32))
                 for r in range(nstream))
    last = lax.fori_loop(0, qi, body, init)
    for r in range(nstream):
        _, acc = absorb(r, qi, *last[r])
        o_ref[r] = (acc[0:dh] / acc[dh:dh + 1]).T.astype(o_ref.dtype)


def moba_attention(proj3, q_off, k_off, v_off, bias_tiles, heads, *, nstream=4, casts=()):
    B, S, _ = proj3.shape
    blk, dh = MOBA_BLOCK, HEAD_DIM
    tq = bias_tiles.shape[-1]
    assert S % tq == 0 and tq % blk == 0 and q_off % dh == 0 and k_off % dh == 0 and v_off % dh == 0
    nstream = nstream if B % nstream == 0 else 1
    nb = S // blk
    nbp = -(-nb // 8) * 8
    assert nbp <= dh, "block one-hot columns must fit beside the keys"
    topk = min(MOBA_TOPK, nb)
    qb, kb_, vb_ = q_off // dh, k_off // dh, v_off // dh
    grid = (heads, B // nstream, S // tq)
    c_ops, c_in, c_shapes, c_out, c_views = _cast_plan(casts, grid)
    kern = functools.partial(_moba_kernel, ncast=len(casts), nb=nb, blk=blk, topk=topk, scale=dh ** -0.5)
    out, *cast = pl.pallas_call(
        kern,
        out_shape=[jax.ShapeDtypeStruct((B, S, heads * dh), BF16)] + c_shapes,
        grid=grid,
        in_specs=[pl.BlockSpec((nstream, tq, dh), lambda h, b, i: (b, i, qb + h)),
                  pl.BlockSpec((nstream, S, dh), lambda h, b, i: (b, 0, kb_ + h),
                               pipeline_mode=pl.Buffered(1)),
                  pl.BlockSpec((nstream, S, dh), lambda h, b, i: (b, 0, vb_ + h),
                               pipeline_mode=pl.Buffered(1)),
                  pl.BlockSpec((None, S // tq, tq, tq), lambda h, b, i: (h, 0, 0, 0),
                               pipeline_mode=pl.Buffered(1))] + c_in,
        out_specs=[pl.BlockSpec((nstream, tq, dh), lambda h, b, i: (b, i, h))] + c_out,
        scratch_shapes=[pltpu.VMEM((nstream, S, 2 * dh), BF16),
                        pltpu.VMEM((nstream, S // tq, MOBA_VT_ROWS, tq), BF16),
                        pltpu.VMEM((nstream, nbp, dh), F32),
                        pltpu.VMEM((2, nstream, tq, tq), F32)],
        compiler_params=_params("parallel", "parallel", "arbitrary"),
        name="moba_attention",
    )(proj3, proj3, proj3, bias_tiles, *c_ops)
    return out, [c.reshape(v) for c, v in zip(cast, c_views)]


def _suffix_sum_matrix(bk):
    r = lax.broadcasted_iota(jnp.int32, (2 * bk, 2 * bk), 0) % bk
    c = lax.broadcasted_iota(jnp.int32, (2 * bk, 2 * bk), 1)
    return jnp.where((c >= bk) | (r > c), -1.0, 0.0).astype(BF16)


def _sb_kernel(*refs, ncast, scale):
    q_ref, k_ref, v_ref = refs[:3]
    o_ref = refs[3 + ncast]
    _cast_slabs(refs[3:3 + ncast], refs[4 + ncast:4 + 2 * ncast])
    qi = pl.program_id(2)
    nbatch, tq, width = q_ref.shape
    dh = HEAD_DIM
    half = tq // 2
    sums = _suffix_sum_matrix(half)
    streams = [(r, c0) for r in range(nbatch) for c0 in range(0, width, dh)]
    nstream = len(streams)
    q2 = [(q_ref[r, :, c0:c0 + dh].astype(F32) * (scale * LOG2E)).astype(BF16) for r, c0 in streams]

    def tile(s, j, run, acc, valid):
        r, c0 = streams[s]
        start = pl.multiple_of(j * tq, tq)
        z = lax.dot_general(q2[s], k_ref[r, pl.ds(start, tq), c0:c0 + dh], _NT,
                            preferred_element_type=F32)
        sp = jnp.maximum(z, 0.0) + jnp.log2(1.0 + jnp.exp2(-jnp.abs(z)))
        neg_log_1m = sp if valid is None else jnp.where(valid, sp, 0.0)
        hi = neg_log_1m.astype(BF16)
        lo = (neg_log_1m - hi.astype(F32)).astype(BF16)
        cs_b = jnp.dot(jnp.concatenate([hi[:, half:], lo[:, half:]], axis=1), sums,
                       preferred_element_type=F32)
        cs_a = jnp.dot(jnp.concatenate([hi[:, :half], lo[:, :half]], axis=1), sums,
                       preferred_element_type=F32)
        run_b = run + cs_b[:, half:]
        after = jnp.concatenate([run_b + cs_a[:, :half], run + cs_b[:, :half]], axis=1)
        a = jnp.exp2((z - sp) + after)
        if valid is not None:
            a = jnp.where(valid, a, 0.0)
        acc = acc + jnp.dot(a.astype(BF16), v_ref[r, pl.ds(start, tq), c0:c0 + dh],
                            preferred_element_type=F32)
        return run_b + cs_a[:, half:], acc

    row = lax.broadcasted_iota(jnp.int32, (tq, tq), 0)
    col = lax.broadcasted_iota(jnp.int32, (tq, tq), 1)
    no_prev = jnp.where(qi > 0, 0.0, 4.0 * EXP2_ZERO_BELOW)
    state = []
    for s in range(nstream):
        run, acc = tile(s, qi, jnp.zeros((tq, half), F32), jnp.zeros((tq, dh), F32), col < row)
        state.append(tile(s, jnp.maximum(qi - 1, 0), run + no_prev, acc, None))

    def run_max(state):
        return functools.reduce(jnp.maximum, [jnp.max(run) for run, _ in state])

    def cond(carry):
        j, _, top = carry
        return (j >= 0) & (top > EXP2_ZERO_BELOW)

    def body(carry):
        j, state, _ = carry
        state = tuple(tile(s, j, run, acc, None) for s, (run, acc) in enumerate(state))
        return j - 1, state, run_max(state)

    _, state, _ = lax.while_loop(cond, body, (qi - 2, tuple(state), run_max(state)))
    for (r, c0), (_, acc) in zip(streams, state):
        o_ref[r, :, c0:c0 + dh] = acc.astype(o_ref.dtype)


def stick_breaking_attention(proj3, q_off, k_off, v_off, heads, *, tq=2 * V7X_LANES, nbatch=4, nhead=2,
                             casts=()):
    B, S, _ = proj3.shape
    dh = HEAD_DIM
    assert tq == 2 * V7X_LANES, "suffix sums work on 128-key halves of a tile"
    nbatch = nbatch if B % nbatch == 0 else 1
    width = nhead * dh
    assert S % tq == 0 and heads % nhead == 0
    assert q_off % width == 0 and k_off % width == 0 and v_off % width == 0
    qb, kb_, vb_ = q_off // width, k_off // width, v_off // width
    grid = (heads // nhead, B // nbatch, S // tq)
    c_ops, c_in, c_shapes, c_out, c_views = _cast_plan(casts, grid)
    out, *cast = pl.pallas_call(
        functools.partial(_sb_kernel, ncast=len(casts), scale=dh ** -0.5),
        out_shape=[jax.ShapeDtypeStruct((B, S, heads * dh), BF16)] + c_shapes,
        grid=grid,
        in_specs=[pl.BlockSpec((nbatch, tq, width), lambda h, b, i: (b, i, qb + h)),
                  pl.BlockSpec((nbatch, S, width), lambda h, b, i: (b, 0, kb_ + h)),
                  pl.BlockSpec((nbatch, S, width), lambda h, b, i: (b, 0, vb_ + h))] + c_in,
        out_specs=[pl.BlockSpec((nbatch, tq, width), lambda h, b, i: (b, i, h))] + c_out,
        compiler_params=_params("parallel", "parallel", "parallel"),
        name="stick_breaking_attention",
    )(proj3, proj3, proj3, *c_ops)
    return out, [c.reshape(v) for c, v in zip(cast, c_views)]


def kernel(x, norm_mix, norm_ffn, w_in, w_pool, pool_scale, w_br_a, w_br_b, w_br_c, w_out, w_gate,
           w_up, w_down, rel_bias, norm_final):
    B, S, D = x.shape
    depth = w_in.shape[0]
    T = B * S
    moba_w = MOBA_HEADS * HEAD_DIM
    sb_w = SB_HEADS * HEAD_DIM
    pool_w = len(POOL_WINDOWS) * POOL_GROUP
    off_qa, off_ka, off_va = 0, moba_w, 2 * moba_w
    off_u = 3 * moba_w
    off_qs = off_u + pool_w
    off_ks, off_vs = off_qs + sb_w, off_qs + 2 * sb_w
    off_gates = off_qs + 3 * sb_w

    bias_tiles = rel_bias_tiles(rel_bias, S // MOBA_QTILE, MOBA_QTILE)
    w_pool = w_pool.astype(BF16)
    h = x.reshape(T, D)
    w_in_l, layer_in = w_in, 0
    for l in range(depth):
        proj = norm_matmul(h, norm_mix[l], w_in_l, layer_in)
        proj3 = proj.reshape(B, S, proj.shape[1])
        moba_casts = [(w_down, l)] + ([(w_in, l + 1)] if l + 1 < depth else [])
        ya, moba_cast = moba_attention(proj3, off_qa, off_ka, off_va, bias_tiles, MOBA_HEADS,
                                       casts=moba_casts)
        yb = pool_mixer(proj, off_u, w_pool, l, pool_scale[l], S)
        sb_casts = [(w, l) for w in (w_gate, w_up, w_br_a, w_br_b, w_br_c, w_out)]
        yc, (w_gate_l, w_up_l, wa_l, wb_l, wc_l, w_out_l) = stick_breaking_attention(
            proj3, off_qs, off_ks, off_vs, SB_HEADS, casts=sb_casts)
        m = merge_branches(ya.reshape(T, moba_w), yb, yc.reshape(T, sb_w), proj, off_gates,
                           wa_l, wb_l, wc_l, 0)
        h = matmul_res(m, w_out_l, 0, h, tn=1024)
        act = ffn_up(h, norm_ffn[l], w_gate_l, w_up_l, 0)
        if l + 1 < depth:
            h = matmul_res(act, moba_cast[0], 0, h)
            w_in_l, layer_in = moba_cast[1], 0
        else:
            h = matmul_res_norm(act, moba_cast[0], 0, h, norm_final)
    return h.reshape(B, S, D)
```

```python
import functools
import math

import jax
import jax.numpy as jnp
from jax import lax
from jax.experimental import pallas as pl
from jax.experimental.pallas import tpu as pltpu

F32 = jnp.float32
BF16 = jnp.bfloat16

HEAD_DIM = 128
MOBA_HEADS = 8
MOBA_BLOCK = 256
MOBA_TOPK = 3
MOBA_QTILE = 512
SB_HEADS = 8
POOL_WINDOWS = (2, 4, 8, 16)
POOL_GROUP = 256
POOL_HALO = 16
REL_BUCKETS = 32
REL_MAX_EXACT = 16
REL_MAX_DIST = 2048
EPS = 1e-6
NEG = -1e30
LOG2E = math.log2(math.e)
EXP2_ZERO_BELOW = -152.0

V7X_LANES = 128
V7X_VMEM_LIMIT_BYTES = 56 * 1024 * 1024

_NT = (((1,), (1,)), ((), ()))


def _params(*semantics):
    return pltpu.CompilerParams(dimension_semantics=semantics,
                                vmem_limit_bytes=V7X_VMEM_LIMIT_BYTES)


def _pick_tile(n, want, align):
    t = min(want, n)
    while t > align and (n % t or t % align):
        t -= align
    assert n % t == 0 and t % align == 0, (n, want, align)
    return t


def _rms_normalize(x, g):
    ms = jnp.mean(x * x, axis=-1, keepdims=True)
    return x * lax.rsqrt(ms + EPS) * g


NORM_CHUNK = 256


def _norm_matmul_kernel(x_ref, g_ref, w_ref, o_ref, xn_ref):
    @pl.when(pl.program_id(1) == 0)
    def _():
        w = w_ref[...].astype(BF16)
        for r0 in range(0, x_ref.shape[0], NORM_CHUNK):
            rows = pl.ds(r0, NORM_CHUNK)
            xn = _rms_normalize(x_ref[rows, :], g_ref[...]).astype(BF16)
            xn_ref[rows, :] = xn
            o_ref[rows, :] = jnp.dot(xn, w, preferred_element_type=F32).astype(o_ref.dtype)

    @pl.when(pl.program_id(1) > 0)
    def _():
        o_ref[...] = jnp.dot(xn_ref[...], w_ref[...].astype(BF16),
                             preferred_element_type=F32).astype(o_ref.dtype)


def norm_matmul(x, g, w, layer, *, out_dtype=BF16, tm=1024, tn=1024):
    T, K = x.shape
    N = w.shape[2]
    tm = _pick_tile(T, tm, 16)
    tn = _pick_tile(N, tn, V7X_LANES)
    return pl.pallas_call(
        _norm_matmul_kernel,
        out_shape=jax.ShapeDtypeStruct((T, N), out_dtype),
        grid=(T // tm, N // tn),
        in_specs=[pl.BlockSpec((tm, K), lambda i, j: (i, 0)),
                  pl.BlockSpec((1, K), lambda i, j: (0, 0)),
                  pl.BlockSpec((None, K, tn), lambda i, j: (layer, 0, j))],
        out_specs=pl.BlockSpec((tm, tn), lambda i, j: (i, j)),
        scratch_shapes=[pltpu.VMEM((tm, K), BF16)],
        compiler_params=_params("parallel", "arbitrary"),
        name="norm_matmul",
    )(x, g.reshape(1, K), w)


def _ffn_up_kernel(x_ref, g_ref, wg_ref, wu_ref, o_ref, xn_ref):
    def swiglu(xn, wg, wu):
        a = jnp.dot(xn, wg, preferred_element_type=F32)
        b = jnp.dot(xn, wu, preferred_element_type=F32)
        return (a * jax.nn.sigmoid(a) * b).astype(o_ref.dtype)

    @pl.when(pl.program_id(1) == 0)
    def _():
        wg, wu = wg_ref[...].astype(BF16), wu_ref[...].astype(BF16)
        for r0 in range(0, x_ref.shape[0], NORM_CHUNK):
            rows = pl.ds(r0, NORM_CHUNK)
            xn = _rms_normalize(x_ref[rows, :], g_ref[...]).astype(BF16)
            xn_ref[rows, :] = xn
            o_ref[rows, :] = swiglu(xn, wg, wu)

    @pl.when(pl.program_id(1) > 0)
    def _():
        o_ref[...] = swiglu(xn_ref[...], wg_ref[...].astype(BF16), wu_ref[...].astype(BF16))


def ffn_up(x, g, wg, wu, layer, *, tm=1024, tn=512):
    T, K = x.shape
    N = wg.shape[2]
    tm = _pick_tile(T, tm, 16)
    tn = _pick_tile(N, tn, V7X_LANES)
    return pl.pallas_call(
        _ffn_up_kernel,
        out_shape=jax.ShapeDtypeStruct((T, N), BF16),
        grid=(T // tm, N // tn),
        in_specs=[pl.BlockSpec((tm, K), lambda i, j: (i, 0)),
                  pl.BlockSpec((1, K), lambda i, j: (0, 0)),
                  pl.BlockSpec((None, K, tn), lambda i, j: (layer, 0, j)),
                  pl.BlockSpec((None, K, tn), lambda i, j: (layer, 0, j))],
        out_specs=pl.BlockSpec((tm, tn), lambda i, j: (i, j)),
        scratch_shapes=[pltpu.VMEM((tm, K), BF16)],
        compiler_params=_params("parallel", "arbitrary"),
        name="ffn_up",
    )(x, g.reshape(1, K), wg, wu)


def _matmul_res_kernel(x_ref, w_ref, r_ref, o_ref):
    o_ref[...] = r_ref[...] + jnp.dot(x_ref[...], w_ref[...], preferred_element_type=F32)


def matmul_res(x, w, layer, res, *, tm=1024, tn=512):
    T, K = x.shape
    N = w.shape[2]
    tm = _pick_tile(T, tm, 16)
    tn = _pick_tile(N, tn, V7X_LANES)
    return pl.pallas_call(
        _matmul_res_kernel,
        out_shape=jax.ShapeDtypeStruct((T, N), F32),
        grid=(T // tm, N // tn),
        in_specs=[pl.BlockSpec((tm, K), lambda i, j: (i, 0)),
                  pl.BlockSpec((None, K, tn), lambda i, j: (layer, 0, j)),
                  pl.BlockSpec((tm, tn), lambda i, j: (i, j))],
        out_specs=pl.BlockSpec((tm, tn), lambda i, j: (i, j)),
        compiler_params=_params("parallel", "parallel"),
        name="matmul_res",
    )(x, w, res)


def _matmul_res_norm_kernel(x_ref, w_ref, r_ref, g_ref, o_ref):
    k = pl.program_id(1)
    last = pl.num_programs(1) - 1

    @pl.when(k == 0)
    def _():
        o_ref[...] = r_ref[...] + jnp.dot(x_ref[...], w_ref[...], preferred_element_type=F32)

    @pl.when((k > 0) & (k < last))
    def _():
        o_ref[...] += jnp.dot(x_ref[...], w_ref[...], preferred_element_type=F32)

    @pl.when(k == last)
    def _():
        w = w_ref[...]
        for r0 in range(0, x_ref.shape[0], NORM_CHUNK):
            rows = pl.ds(r0, NORM_CHUNK)
            h = o_ref[rows, :] + jnp.dot(x_ref[rows, :], w, preferred_element_type=F32)
            o_ref[rows, :] = _rms_normalize(h, g_ref[...])


def matmul_res_norm(x, w, layer, res, g, *, tm=1024, tk=512):
    T, K = x.shape
    N = w.shape[2]
    tm = _pick_tile(T, tm, 16)
    tk = _pick_tile(K, tk, V7X_LANES)
    assert K // tk >= 2 and tm % NORM_CHUNK == 0, "first and last contraction steps must differ"
    return pl.pallas_call(
        _matmul_res_norm_kernel,
        out_shape=jax.ShapeDtypeStruct((T, N), F32),
        grid=(T // tm, K // tk),
        in_specs=[pl.BlockSpec((tm, tk), lambda i, k: (i, k)),
                  pl.BlockSpec((None, tk, N), lambda i, k: (layer, k, 0)),
                  pl.BlockSpec((tm, N), lambda i, k: (i, 0)),
                  pl.BlockSpec((1, N), lambda i, k: (0, 0))],
        out_specs=pl.BlockSpec((tm, N), lambda i, k: (i, 0)),
        compiler_params=_params("parallel", "arbitrary"),
        name="matmul_res_norm",
    )(x, w, res, g.reshape(1, N))


def _merge_kernel(ya_ref, yb_ref, yc_ref, ga_ref, gb_ref, gc_ref, wa_ref, wb_ref, wc_ref, o_ref):
    def branch(y_ref, g_ref, w_ref):
        gate = jax.nn.sigmoid(g_ref[...].astype(F32))
        return gate * jnp.dot(y_ref[...], w_ref[...], preferred_element_type=F32)

    m = branch(ya_ref, ga_ref, wa_ref) + branch(yb_ref, gb_ref, wb_ref) + branch(yc_ref, gc_ref, wc_ref)
    o_ref[...] = m.astype(o_ref.dtype)


def merge_branches(ya, yb, yc, proj, gate_off, wa, wb, wc, layer, *, tm=1024, tn=1024):
    T, K = ya.shape
    N = wa.shape[2]
    tm = _pick_tile(T, tm, 16)
    tn = _pick_tile(N, tn, V7X_LANES)
    assert gate_off % tn == 0
    y_spec = pl.BlockSpec((tm, K), lambda i, j: (i, 0))
    w_spec = pl.BlockSpec((None, K, tn), lambda i, j: (layer, 0, j))

    def gate_spec(branch):
        base = (gate_off + branch * N) // tn
        return pl.BlockSpec((tm, tn), lambda i, j: (i, base + j))

    return pl.pallas_call(
        _merge_kernel,
        out_shape=jax.ShapeDtypeStruct((T, N), BF16),
        grid=(T // tm, N // tn),
        in_specs=[y_spec, y_spec, y_spec, gate_spec(0), gate_spec(1), gate_spec(2),
                  w_spec, w_spec, w_spec],
        out_specs=pl.BlockSpec((tm, tn), lambda i, j: (i, j)),
        compiler_params=_params("parallel", "parallel"),
        name="merge_branches",
    )(ya, yb, yc, proj, proj, proj, wa, wb, wc)


POOL_PAD = 8


def _pool_kernel(u_ref, halo_ref, w_ref, sc_ref, o_ref, ext_ref, stage_ref, *, tm, tiles_per_seq):
    t_in_seq = pl.program_id(0) % tiles_per_seq
    base = POOL_PAD + POOL_HALO
    rows = tm + POOL_HALO
    zeros = jnp.zeros((POOL_PAD, ext_ref.shape[1]), F32)
    ext_ref[0:POOL_PAD, :] = zeros
    ext_ref[POOL_PAD:base, :] = jnp.where(t_in_seq == 0, 0.0, halo_ref[...].astype(F32))
    ext_ref[base:, :] = u_ref[...].astype(F32)
    stage_ref[0:POOL_PAD, :] = zeros[:, 0:POOL_GROUP]
    pos = t_in_seq * tm + lax.broadcasted_iota(jnp.int32, (tm, 1), 0)
    for gi, win in enumerate(POOL_WINDOWS):
        cols = slice(gi * POOL_GROUP, (gi + 1) * POOL_GROUP)
        x = ext_ref[base:, cols]
        cur = ext_ref[pl.ds(POOL_PAD, rows), cols] + ext_ref[pl.ds(POOL_PAD - 1, rows), cols]
        span = 2
        while span < win:
            stage_ref[pl.ds(POOL_PAD, rows), :] = cur
            cur = cur + stage_ref[pl.ds(POOL_PAD - span, rows), :]
            span *= 2
        assert span == win <= 2 * POOL_PAD, "pooling windows must be powers of two up to 2 * POOL_PAD"
        total = cur[POOL_HALO:]
        cnt = jnp.minimum(pos + 1, win).astype(F32)
        pooled = total / cnt - x
        y = jnp.dot(pooled.astype(BF16), w_ref[gi], preferred_element_type=F32)
        o_ref[:, cols] = (y * sc_ref[:, cols]).astype(o_ref.dtype)


def pool_mixer(proj, u_off, w_grp, layer, scale, seq_len, *, tm=1024):
    T = proj.shape[0]
    width = len(POOL_WINDOWS) * POOL_GROUP
    tm = _pick_tile(seq_len, tm, POOL_HALO)
    assert u_off % width == 0 and max(POOL_WINDOWS) <= POOL_HALO
    ublk = u_off // width
    rows_per_tile = tm // POOL_HALO
    return pl.pallas_call(
        functools.partial(_pool_kernel, tm=tm, tiles_per_seq=seq_len // tm),
        out_shape=jax.ShapeDtypeStruct((T, width), BF16),
        grid=(T // tm,),
        in_specs=[pl.BlockSpec((tm, width), lambda i: (i, ublk)),
                  pl.BlockSpec((POOL_HALO, width),
                               lambda i: (jnp.maximum(i * rows_per_tile - 1, 0), ublk)),
                  pl.BlockSpec((None,) + w_grp.shape[1:], lambda i: (layer, 0, 0, 0)),
                  pl.BlockSpec((1, width), lambda i: (0, 0))],
        out_specs=pl.BlockSpec((tm, width), lambda i: (i, 0)),
        scratch_shapes=[pltpu.VMEM((POOL_PAD + POOL_HALO + tm, width), F32),
                        pltpu.VMEM((POOL_PAD + POOL_HALO + tm, POOL_GROUP), F32)],
        compiler_params=_params("parallel"),
        name="pool_mixer",
    )(proj, proj, w_grp, scale.reshape(1, width))


BF16_ROWS = 16


def _cast_plan(casts, grid):
    nsteps = grid[0] * grid[1] * grid[2]
    step = lambda h, b, i: (h * grid[1] + b) * grid[2] + i
    operands, in_specs, out_shapes, out_specs, shapes = [], [], [], [], []
    for w, layer in casts:
        depth, K, N = w.shape
        share = 1
        while (K * share) % (nsteps * BF16_ROWS) and share < nsteps:
            share *= 2
        assert (K * share) % (nsteps * BF16_ROWS) == 0 and nsteps % share == 0, (w.shape, nsteps)
        rows = K * share // nsteps
        operands.append(w)
        in_specs.append(pl.BlockSpec((None, rows, N), lambda h, b, i, layer=layer, share=share:
                                     (layer, step(h, b, i) // share, 0)))
        out_shapes.append(jax.ShapeDtypeStruct((K, N), BF16))
        out_specs.append(pl.BlockSpec((rows, N), lambda h, b, i, share=share: (step(h, b, i) // share, 0)))
        shapes.append((1, K, N))
    return operands, in_specs, out_shapes, out_specs, shapes


def _cast_slabs(src_refs, dst_refs):
    for src, dst in zip(src_refs, dst_refs):
        dst[...] = src[...].astype(BF16)


def _bucket_range(d_lo, d_hi):
    def bucket(d):
        if d < REL_MAX_EXACT:
            return d
        big = REL_MAX_EXACT + int(math.log(d / REL_MAX_EXACT) / math.log(REL_MAX_DIST / REL_MAX_EXACT)
                                  * (REL_BUCKETS - REL_MAX_EXACT))
        return min(big, REL_BUCKETS - 1)
    return max(bucket(max(d_lo, 0)) - 1, 0), min(bucket(d_hi) + 1, REL_BUCKETS - 1)


BIAS_ROWS = 8


def _bias_tiles_kernel(tab_ref, o_ref, *, heads, blk, ndist):
    key0 = lax.broadcasted_iota(jnp.int32, (BIAS_ROWS, blk), 0)
    qry = lax.broadcasted_iota(jnp.int32, (BIAS_ROWS, blk), 1)
    for dist in range(ndist):
        @pl.when(pl.program_id(0) == dist)
        def _(dist=dist):
            b_lo, b_hi = _bucket_range(dist * blk - (blk - 1), dist * blk + (blk - 1))
            table = [[tab_ref[h * REL_BUCKETS + b] * LOG2E for b in range(b_lo, b_hi + 1)]
                     for h in range(heads)]

            def rows(c, carry):
                r0 = pl.multiple_of(c * BIAS_ROWS, BIAS_ROWS)
                d = dist * blk + qry - (key0 + r0)
                n = jnp.maximum(d, 0)
                nf = jnp.maximum(n, 1).astype(F32)
                large = REL_MAX_EXACT + (jnp.log(nf / REL_MAX_EXACT)
                                         / math.log(REL_MAX_DIST / REL_MAX_EXACT)
                                         * (REL_BUCKETS - REL_MAX_EXACT)).astype(jnp.int32)
                large = jnp.minimum(large, REL_BUCKETS - 1)
                bucket = jnp.where(n < REL_MAX_EXACT, n, large)
                accs = [jnp.full((BIAS_ROWS, blk), table[h][-1], F32) for h in range(heads)]
                for i, b in enumerate(range(b_lo, b_hi)):
                    mask = bucket == b
                    accs = [jnp.where(mask, table[h][i], accs[h]) for h in range(heads)]
                for h in range(heads):
                    val = jnp.where(d >= 0, accs[h], NEG) if dist == 0 else accs[h]
                    o_ref[h, 0, pl.ds(r0, BIAS_ROWS), :] = val
                return carry

            lax.fori_loop(0, blk // BIAS_ROWS, rows, 0)


def rel_bias_tiles(rel_table, ndist, blk):
    heads = rel_table.shape[0]
    return pl.pallas_call(
        functools.partial(_bias_tiles_kernel, heads=heads, blk=blk, ndist=ndist),
        out_shape=jax.ShapeDtypeStruct((heads, ndist, blk, blk), F32),
        grid=(ndist,),
        in_specs=[pl.BlockSpec(memory_space=pltpu.SMEM)],
        out_specs=pl.BlockSpec((heads, 1, blk, blk), lambda d: (0, d, 0, 0)),
        compiler_params=_params("parallel"),
        name="rel_bias_tiles",
    )(rel_table.reshape(-1))


MOBA_VT_ROWS = HEAD_DIM + 8


def _moba_kernel(*refs, ncast, nb, blk, topk, scale):
    q_ref, k_ref, v_ref, bias_ref = refs[:4]
    o_ref = refs[4 + ncast]
    kaug_ref, vt_ref, kbar_ref, s_ref = refs[5 + 2 * ncast:]
    _cast_slabs(refs[4:4 + ncast], refs[5 + ncast:5 + 2 * ncast])
    qi = pl.program_id(2)
    nstream, tq, dh = q_ref.shape
    seq = k_ref.shape[1]
    nbp = kbar_ref.shape[1]

    @pl.when(qi == 0)
    def _():
        row_blk = lax.broadcasted_iota(jnp.int32, (seq, dh), 0) // blk
        col = lax.broadcasted_iota(jnp.int32, (seq, dh), 1)
        onehot = (row_blk == col).astype(BF16)
        ones_rows = (lax.broadcasted_iota(jnp.int32, (MOBA_VT_ROWS - dh, tq), 0) == 0).astype(BF16)
        for r in range(nstream):
            kaug_ref[r, :, 0:dh] = (k_ref[r].astype(F32) * (scale * LOG2E)).astype(BF16)
            kaug_ref[r, :, dh:] = onehot
            for t in range(seq // tq):
                vt_ref[r, t, 0:dh, :] = v_ref[r, t * tq:(t + 1) * tq, :].astype(F32).T.astype(BF16)
                vt_ref[r, t, dh:, :] = ones_rows
            kbar_ref[r] = jnp.zeros(kbar_ref.shape[1:], F32)
            for n in range(nb):
                kblk = k_ref[r, n * blk:(n + 1) * blk, :].astype(F32)
                kbar_ref[r, n:n + 1, :] = jnp.mean(kblk, axis=0, keepdims=True)

    blk_id = lax.broadcasted_iota(jnp.int32, (nbp, tq), 0)
    own = qi * (tq // blk) + lax.broadcasted_iota(jnp.int32, (nbp, tq), 1) // blk
    past = blk_id < own

    def augmented_query(r):
        q = q_ref[r]
        kbar = kbar_ref[r]
        kbar_hi = kbar.astype(BF16)
        kbar_lo = (kbar - kbar_hi.astype(F32)).astype(BF16)
        gate = (lax.dot_general(kbar_hi, q, _NT, preferred_element_type=F32)
                + lax.dot_general(kbar_lo, q, _NT, preferred_element_type=F32))
        gate = jnp.where(past, gate, -jnp.inf)
        rank = jnp.zeros((nbp, tq), jnp.int32)
        for m in range(nb - 1):
            gm = gate[m:m + 1, :]
            before = (gm > gate) | ((gm == gate) & (m < blk_id))
            rank = rank + before.astype(jnp.int32)
        chosen = (past & (rank < topk)) | (blk_id == own)
        pen_t = jnp.where(chosen, 0.0, NEG)
        pen_t = jnp.concatenate([pen_t, jnp.zeros((dh - nbp, tq), F32)], axis=0)
        return jnp.concatenate([q, pen_t.T.astype(BF16)], axis=1)

    q_aug = [augmented_query(r) for r in range(nstream)]

    def produce(r, dist):
        start = pl.multiple_of((qi - dist) * tq, tq)
        s = lax.dot_general(kaug_ref[r, pl.ds(start, tq), :], q_aug[r], _NT,
                            preferred_element_type=F32) + bias_ref[dist]
        s_ref[dist % 2, r] = s
        return jnp.max(s, axis=0, keepdims=True)

    def absorb(r, dist, s_max, m_i, acc):
        m_new = jnp.maximum(m_i, s_max)
        p = jnp.exp2(s_ref[dist % 2, r] - m_new)
        acc = (jnp.exp2(m_i - m_new) * acc
               + jnp.dot(vt_ref[r, qi - dist], p.astype(BF16), preferred_element_type=F32))
        return m_new, acc

    def body(dist, carry):
        done = [absorb(r, dist, *carry[r]) for r in range(nstream)]
        return tuple((produce(r, dist + 1),) + done[r] for r in range(nstream))

    init = tuple((produce(r, 0), jnp.full((1, tq), NEG, F32), jnp.zeros((MOBA_VT_ROWS, tq), F32))
                 for r in range(nstream))
    last = lax.fori_loop(0, qi, body, init)
    for r in range(nstream):
        _, acc = absorb(r, qi, *last[r])
        o_ref[r] = (acc[0:dh] / acc[dh:dh + 1]).T.astype(o_ref.dtype)


def moba_attention(proj3, q_off, k_off, v_off, bias_tiles, heads, *, nstream=4, casts=()):
    B, S, _ = proj3.shape
    blk, dh = MOBA_BLOCK, HEAD_DIM
    tq = bias_tiles.shape[-1]
    assert S % tq == 0 and tq % blk == 0 and q_off % dh == 0 and k_off % dh == 0 and v_off % dh == 0
    nstream = nstream if B % nstream == 0 else 1
    nb = S // blk
    nbp = -(-nb // 8) * 8
    assert nbp <= dh, "block one-hot columns must fit beside the keys"
    topk = min(MOBA_TOPK, nb)
    qb, kb_, vb_ = q_off // dh, k_off // dh, v_off // dh
    grid = (heads, B // nstream, S // tq)
    c_ops, c_in, c_shapes, c_out, c_views = _cast_plan(casts, grid)
    kern = functools.partial(_moba_kernel, ncast=len(casts), nb=nb, blk=blk, topk=topk, scale=dh ** -0.5)
    out, *cast = pl.pallas_call(
        kern,
        out_shape=[jax.ShapeDtypeStruct((B, S, heads * dh), BF16)] + c_shapes,
        grid=grid,
        in_specs=[pl.BlockSpec((nstream, tq, dh), lambda h, b, i: (b, i, qb + h)),
                  pl.BlockSpec((nstream, S, dh), lambda h, b, i: (b, 0, kb_ + h)),
                  pl.BlockSpec((nstream, S, dh), lambda h, b, i: (b, 0, vb_ + h)),
                  pl.BlockSpec((None, S // tq, tq, tq), lambda h, b, i: (h, 0, 0, 0),
                               pipeline_mode=pl.Buffered(1))] + c_in,
        out_specs=[pl.BlockSpec((nstream, tq, dh), lambda h, b, i: (b, i, h))] + c_out,
        scratch_shapes=[pltpu.VMEM((nstream, S, 2 * dh), BF16),
                        pltpu.VMEM((nstream, S // tq, MOBA_VT_ROWS, tq), BF16),
                        pltpu.VMEM((nstream, nbp, dh), F32),
                        pltpu.VMEM((2, nstream, tq, tq), F32)],
        compiler_params=_params("parallel", "parallel", "arbitrary"),
        name="moba_attention",
    )(proj3, proj3, proj3, bias_tiles, *c_ops)
    return out, [c.reshape(v) for c, v in zip(cast, c_views)]


def _suffix_sum_matrix(bk):
    r = lax.broadcasted_iota(jnp.int32, (2 * bk, 2 * bk), 0) % bk
    c = lax.broadcasted_iota(jnp.int32, (2 * bk, 2 * bk), 1)
    return jnp.where((c >= bk) | (r > c), -1.0, 0.0).astype(BF16)


def _sb_kernel(*refs, ncast, scale):
    q_ref, k_ref, v_ref = refs[:3]
    o_ref = refs[3 + ncast]
    _cast_slabs(refs[3:3 + ncast], refs[4 + ncast:4 + 2 * ncast])
    qi = pl.program_id(2)
    nbatch, tq, width = q_ref.shape
    dh = HEAD_DIM
    half = tq // 2
    sums = _suffix_sum_matrix(half)
    streams = [(r, c0) for r in range(nbatch) for c0 in range(0, width, dh)]
    nstream = len(streams)
    q2 = [(q_ref[r, :, c0:c0 + dh].astype(F32) * (scale * LOG2E)).astype(BF16) for r, c0 in streams]

    def tile(s, j, run, acc, valid):
        r, c0 = streams[s]
        start = pl.multiple_of(j * tq, tq)
        z = lax.dot_general(q2[s], k_ref[r, pl.ds(start, tq), c0:c0 + dh], _NT,
                            preferred_element_type=F32)
        sp = jnp.maximum(z, 0.0) + jnp.log2(1.0 + jnp.exp2(-jnp.abs(z)))
        neg_log_1m = sp if valid is None else jnp.where(valid, sp, 0.0)
        hi = neg_log_1m.astype(BF16)
        lo = (neg_log_1m - hi.astype(F32)).astype(BF16)
        cs_b = jnp.dot(jnp.concatenate([hi[:, half:], lo[:, half:]], axis=1), sums,
                       preferred_element_type=F32)
        cs_a = jnp.dot(jnp.concatenate([hi[:, :half], lo[:, :half]], axis=1), sums,
                       preferred_element_type=F32)
        run_b = run + cs_b[:, half:]
        after = jnp.concatenate([run_b + cs_a[:, :half], run + cs_b[:, :half]], axis=1)
        a = jnp.exp2((z - sp) + after)
        if valid is not None:
            a = jnp.where(valid, a, 0.0)
        acc = acc + jnp.dot(a.astype(BF16), v_ref[r, pl.ds(start, tq), c0:c0 + dh],
                            preferred_element_type=F32)
        return run_b + cs_a[:, half:], acc

    row = lax.broadcasted_iota(jnp.int32, (tq, tq), 0)
    col = lax.broadcasted_iota(jnp.int32, (tq, tq), 1)
    no_prev = jnp.where(qi > 0, 0.0, 4.0 * EXP2_ZERO_BELOW)
    state = []
    for s in range(nstream):
        run, acc = tile(s, qi, jnp.zeros((tq, half), F32), jnp.zeros((tq, dh), F32), col < row)
        state.append(tile(s, jnp.maximum(qi - 1, 0), run + no_prev, acc, None))

    def run_max(state):
        return functools.reduce(jnp.maximum, [jnp.max(run) for run, _ in state])

    def cond(carry):
        j, _, top = carry
        return (j >= 0) & (top > EXP2_ZERO_BELOW)

    def body(carry):
        j, state, _ = carry
        state = tuple(tile(s, j, run, acc, None) for s, (run, acc) in enumerate(state))
        return j - 1, state, run_max(state)

    _, state, _ = lax.while_loop(cond, body, (qi - 2, tuple(state), run_max(state)))
    for (r, c0), (_, acc) in zip(streams, state):
        o_ref[r, :, c0:c0 + dh] = acc.astype(o_ref.dtype)


def stick_breaking_attention(proj3, q_off, k_off, v_off, heads, *, tq=2 * V7X_LANES, nbatch=4, nhead=2,
                             casts=()):
    B, S, _ = proj3.shape
    dh = HEAD_DIM
    assert tq == 2 * V7X_LANES, "suffix sums work on 128-key halves of a tile"
    nbatch = nbatch if B % nbatch == 0 else 1
    width = nhead * dh
    assert S % tq == 0 and heads % nhead == 0
    assert q_off % width == 0 and k_off % width == 0 and v_off % width == 0
    qb, kb_, vb_ = q_off // width, k_off // width, v_off // width
    grid = (heads // nhead, B // nbatch, S // tq)
    c_ops, c_in, c_shapes, c_out, c_views = _cast_plan(casts, grid)
    out, *cast = pl.pallas_call(
        functools.partial(_sb_kernel, ncast=len(casts), scale=dh ** -0.5),
        out_shape=[jax.ShapeDtypeStruct((B, S, heads * dh), BF16)] + c_shapes,
        grid=grid,
        in_specs=[pl.BlockSpec((nbatch, tq, width), lambda h, b, i: (b, i, qb + h)),
                  pl.BlockSpec((nbatch, S, width), lambda h, b, i: (b, 0, kb_ + h)),
                  pl.BlockSpec((nbatch, S, width), lambda h, b, i: (b, 0, vb_ + h))] + c_in,
        out_specs=[pl.BlockSpec((nbatch, tq, width), lambda h, b, i: (b, i, h))] + c_out,
        compiler_params=_params("parallel", "parallel", "parallel"),
        name="stick_breaking_attention",
    )(proj3, proj3, proj3, *c_ops)
    return out, [c.reshape(v) for c, v in zip(cast, c_views)]


def kernel(x, norm_mix, norm_ffn, w_in, w_pool, pool_scale, w_br_a, w_br_b, w_br_c, w_out, w_gate,
           w_up, w_down, rel_bias, norm_final):
    B, S, D = x.shape
    depth = w_in.shape[0]
    T = B * S
    moba_w = MOBA_HEADS * HEAD_DIM
    sb_w = SB_HEADS * HEAD_DIM
    pool_w = len(POOL_WINDOWS) * POOL_GROUP
    off_qa, off_ka, off_va = 0, moba_w, 2 * moba_w
    off_u = 3 * moba_w
    off_qs = off_u + pool_w
    off_ks, off_vs = off_qs + sb_w, off_qs + 2 * sb_w
    off_gates = off_qs + 3 * sb_w

    bias_tiles = rel_bias_tiles(rel_bias, S // MOBA_QTILE, MOBA_QTILE)
    w_pool = w_pool.astype(BF16)
    h = x.reshape(T, D)
    w_in_l, layer_in = w_in, 0
    for l in range(depth):
        proj = norm_matmul(h, norm_mix[l], w_in_l, layer_in)
        proj3 = proj.reshape(B, S, proj.shape[1])
        ya, moba_cast = moba_attention(proj3, off_qa, off_ka, off_va, bias_tiles, MOBA_HEADS,
                                       casts=[(w_down, l)])
        yb = pool_mixer(proj, off_u, w_pool, l, pool_scale[l], S)
        sb_casts = [(w, l) for w in (w_gate, w_up, w_br_a, w_br_b, w_br_c, w_out)]
        sb_casts += [(w_in, l + 1)] if l + 1 < depth else []
        yc, (w_gate_l, w_up_l, wa_l, wb_l, wc_l, w_out_l, *w_in_next) = stick_breaking_attention(
            proj3, off_qs, off_ks, off_vs, SB_HEADS, casts=sb_casts)
        m = merge_branches(ya.reshape(T, moba_w), yb, yc.reshape(T, sb_w), proj, off_gates,
                           wa_l, wb_l, wc_l, 0)
        h = matmul_res(m, w_out_l, 0, h, tn=1024)
        act = ffn_up(h, norm_ffn[l], w_gate_l, w_up_l, 0)
        if l + 1 < depth:
            h = matmul_res(act, moba_cast[0], 0, h)
            w_in_l, layer_in = w_in_next[0], 0
        else:
            h = matmul_res_norm(act, moba_cast[0], 0, h, norm_final)
    return h.reshape(B, S, D)
```
